```python
import jax, jax.numpy as jnp
from jax import lax
import numpy as np

D_MODEL = 1024
BATCH = 16
SEQ = 4096
DEPTH = 1
DEC_BATCH = 8
DEC_SEQ = 16
PAST_LEN = 4096

CHUNK = 64
Q_BLOCK = 128
H_ATT = 8
HEAD_DIM = 64
D_ATT = H_ATT * HEAD_DIM
H_SSD = 8
SSD_HEAD_DIM = 64
D_SSD = H_SSD * SSD_HEAD_DIM
SSD_GROUPS = 2
HEADS_PER_GROUP = H_SSD // SSD_GROUPS
SSD_STATE = 64
CONV_W = 4
CONV_DIM = D_SSD + 2 * SSD_GROUPS * SSD_STATE
D_IN = 3 * D_ATT + D_SSD + CONV_DIM + H_SSD
N_KEYS = 128
N_EXPERTS = N_KEYS * N_KEYS
PEER_HEADS = 8
PEER_TOPK = 16
PEER_QDIM = 128
PEER_HALF = PEER_QDIM // 2
PEER_BLOCK = 256
EPS = 1e-6

kernel_name = "hybrid_stickbreak_ssd_peer_stream_step"


def rmsnorm(x, g):
    xf = x.astype(jnp.float32)
    return (xf * lax.rsqrt(jnp.mean(xf * xf, axis=-1, keepdims=True) + EPS) * g).astype(x.dtype)


def stick_breaking(q, k, v, q_pos, k_pos):
    z = jnp.einsum('bqhd,bkhd->bhqk', q, k).astype(jnp.float32) * (HEAD_DIM ** -0.5)
    valid = k_pos[None, :] < q_pos[:, None]
    log_keep = jnp.where(valid, jax.nn.log_sigmoid(-z), 0.0)
    after = lax.cumsum(log_keep, axis=3, reverse=True) - log_keep
    w = jnp.where(valid, jnp.exp(jax.nn.log_sigmoid(z) + after), 0.0)
    return jnp.einsum('bhqk,bkhd->bqhd', w.astype(v.dtype), v)


def ssd_scan(xs, dt, a, bm, cm, h0, q):
    b, l, g, kk, p = xs.shape
    nc = l // q
    xs = xs.reshape(b, nc, q, g, kk, p)
    dt = dt.reshape(b, nc, q, g, kk)
    bm = bm.reshape(b, nc, q, g, -1)
    cm = cm.reshape(b, nc, q, g, -1)
    cum = jnp.cumsum(dt * a, axis=2)
    causal = jnp.tril(jnp.ones((q, q), bool))
    seg = cum[:, :, :, None] - cum[:, :, None, :]
    decay = jnp.exp(jnp.where(causal[:, :, None, None], seg, -jnp.inf))
    xdt = xs * dt[..., None]
    cb = jnp.einsum('bctgn,bcsgn->bctsg', cm, bm)
    y_diag = jnp.einsum('bctsg,bctsgk,bcsgkp->bctgkp', cb, decay, xdt)
    to_end = jnp.exp(cum[:, :, -1:] - cum)
    chunk_states = jnp.einsum('bcsgn,bcsgk,bcsgkp->bcgkpn', bm, to_end, xdt)
    chunk_decay = jnp.exp(cum[:, :, -1])

    def step(h, inp):
        s, d = inp
        return d[..., None, None] * h + s, h

    h_fin, h_in = lax.scan(step, h0.astype(jnp.float32),
                           (jnp.moveaxis(chunk_states, 1, 0), jnp.moveaxis(chunk_decay, 1, 0)))
    h_in = jnp.moveaxis(h_in, 0, 1)
    y_off = jnp.einsum('bctgn,bcgkpn,bctgk->bctgkp', cm, h_in, jnp.exp(cum))
    return (y_diag + y_off).reshape(b, l, g, kk, p), h_fin


def peer_ffn(h, w_pq, sub_keys, u_tab, v_tab):
    b, l, d = h.shape
    n = b * l
    nb = -(-n // PEER_BLOCK)
    t = jnp.pad(h.reshape(n, d), ((0, nb * PEER_BLOCK - n), (0, 0))).reshape(nb, PEER_BLOCK, d)

    def block(xb):
        qh = (xb @ w_pq).reshape(PEER_BLOCK, PEER_HEADS, 2, PEER_HALF)
        s = jnp.einsum('thcd,ckd->thck', qh, sub_keys).astype(jnp.float32)
        s_top, i_top = lax.top_k(s, PEER_TOPK)
        cand = (s_top[:, :, 0, :, None] + s_top[:, :, 1, None, :]).reshape(PEER_BLOCK, PEER_HEADS, PEER_TOPK * PEER_TOPK)
        best, ci = lax.top_k(cand, PEER_TOPK)
        e = (jnp.take_along_axis(i_top[:, :, 0], ci // PEER_TOPK, axis=-1) * N_KEYS
             + jnp.take_along_axis(i_top[:, :, 1], ci % PEER_TOPK, axis=-1))
        gw = jax.nn.softmax(best, axis=-1).reshape(PEER_BLOCK, PEER_HEADS * PEER_TOPK)
        e = e.reshape(PEER_BLOCK, PEER_HEADS * PEER_TOPK)
        act = jax.nn.gelu(jnp.einsum('tkd,td->tk', u_tab[e], xb).astype(jnp.float32)) * gw
        return jnp.einsum('tk,tkd->td', act.astype(xb.dtype), v_tab[e])

    y = lax.map(block, t).reshape(nb * PEER_BLOCK, d)[:n]
    return y.reshape(b, l, d)


def _layer(x, c, k_past, v_past, ssm0, conv0, w_ada, b_ada, g_pre_mix, g_post_mix, g_pre_ffn,
           g_post_ffn, w_in, conv_w, conv_b, dt_bias, a_log, d_skip, g_attn_norm, g_ssd_norm,
           w_out, w_pq, sub_keys, u_tab, v_tab):
    b, l, _ = x.shape
    mod = jax.nn.silu(c) @ w_ada + b_ada
    sh1, sc1, gt1, sh2, sc2, gt2 = [m[:, None, :] for m in jnp.split(mod, 6, axis=-1)]

    h = rmsnorm(x, g_pre_mix) * (1 + sc1) + sh1
    proj = h @ w_in
    cuts = [D_ATT, 2 * D_ATT, 3 * D_ATT, 3 * D_ATT + D_SSD, 3 * D_ATT + D_SSD + CONV_DIM]
    q, k, v, z, xbc, dt_raw = jnp.split(proj, cuts, axis=-1)

    q = q.reshape(b, l, H_ATT, HEAD_DIM)
    k = k.reshape(b, l, H_ATT, HEAD_DIM)
    v = v.reshape(b, l, H_ATT, HEAD_DIM)
    if k_past is None:
        pos = jnp.arange(l)
        outs = [stick_breaking(q[:, s:s + Q_BLOCK], k[:, :s + Q_BLOCK], v[:, :s + Q_BLOCK],
                               pos[s:s + Q_BLOCK], pos[:s + Q_BLOCK])
                for s in range(0, l, Q_BLOCK)]
        o = jnp.concatenate(outs, axis=1)
    else:
        past = k_past.shape[1]
        k_all = jnp.concatenate([k_past.astype(k.dtype), k], axis=1)
        v_all = jnp.concatenate([v_past.astype(v.dtype), v], axis=1)
        o = stick_breaking(q, k_all, v_all, past + jnp.arange(l), jnp.arange(past + l))
    attn_o = rmsnorm(o.reshape(b, l, D_ATT), g_attn_norm)

    conv_in = jnp.concatenate([conv0.astype(xbc.dtype), xbc], axis=1)
    conv_new = conv_in[:, -(CONV_W - 1):]
    xbc = lax.conv_general_dilated(conv_in, conv_w.astype(conv_in.dtype), (1,), 'VALID',
                                   dimension_numbers=('NWC', 'WIO', 'NWC'),
                                   feature_group_count=CONV_DIM)
    xbc = jax.nn.silu(xbc + conv_b)
    xs, bm, cm = jnp.split(xbc, [D_SSD, D_SSD + SSD_GROUPS * SSD_STATE], axis=-1)
    xs = xs.reshape(b, l, SSD_GROUPS, HEADS_PER_GROUP, SSD_HEAD_DIM)
    bm = bm.reshape(b, l, SSD_GROUPS, SSD_STATE)
    cm = cm.reshape(b, l, SSD_GROUPS, SSD_STATE)
    dt = jax.nn.softplus((dt_raw + dt_bias).astype(jnp.float32)).reshape(b, l, SSD_GROUPS, HEADS_PER_GROUP)
    a = -jnp.exp(a_log.astype(jnp.float32)).reshape(SSD_GROUPS, HEADS_PER_GROUP)
    h0 = ssm0.reshape(b, SSD_GROUPS, HEADS_PER_GROUP, SSD_HEAD_DIM, SSD_STATE)
    chunk = CHUNK if l % CHUNK == 0 else l
    y, h_fin = ssd_scan(xs, dt, a, bm, cm, h0, chunk)
    y = y + d_skip.reshape(SSD_GROUPS, HEADS_PER_GROUP, 1) * xs
    ssd_o = rmsnorm((y.reshape(b, l, D_SSD) * jax.nn.silu(z)).astype(x.dtype), g_ssd_norm)

    mix = jnp.concatenate([attn_o, ssd_o], axis=-1) @ w_out
    x = x + gt1 * rmsnorm(mix, g_post_mix)

    h2 = rmsnorm(x, g_pre_ffn) * (1 + sc2) + sh2
    f = peer_ffn(h2, w_pq, sub_keys, u_tab, v_tab)
    x = x + gt2 * rmsnorm(f, g_post_ffn)
    ssm_new = h_fin.reshape(b, H_SSD, SSD_HEAD_DIM, SSD_STATE).astype(ssm0.dtype)
    return x, k, v, ssm_new, conv_new


def setup_inputs(seed: int = 0) -> dict:
    key = jax.random.key(seed)
    ks = jax.random.split(key, 32)
    f32 = jnp.float32

    def nrm(k, shape, scale):
        return jax.random.normal(k, shape, f32) * scale

    L = DEPTH
    dt0 = jnp.exp(jax.random.uniform(ks[20], (L, H_SSD), f32, np.log(1e-3), np.log(1e-1)))
    return {
        "x_prompt": nrm(ks[0], (BATCH, SEQ, D_MODEL), 1.0),
        "x_sample": nrm(ks[1], (DEC_BATCH, DEC_SEQ, D_MODEL), 1.0),
        "c_prompt": nrm(ks[2], (BATCH, D_MODEL), 1.0),
        "c_sample": nrm(ks[3], (DEC_BATCH, D_MODEL), 1.0),
        "cache_k": nrm(ks[4], (L, DEC_BATCH, PAST_LEN, H_ATT, HEAD_DIM), 1.0),
        "cache_v": nrm(ks[5], (L, DEC_BATCH, PAST_LEN, H_ATT, HEAD_DIM), 1.0),
        "state_ssm": nrm(ks[6], (L, DEC_BATCH, H_SSD, SSD_HEAD_DIM, SSD_STATE), 0.5),
        "state_conv": nrm(ks[7], (L, DEC_BATCH, CONV_W - 1, CONV_DIM), 1.0),
        "w_ada": nrm(ks[8], (L, D_MODEL, 6 * D_MODEL), 0.5 * D_MODEL ** -0.5),
        "b_ada": nrm(ks[9], (L, 6 * D_MODEL), 0.01),
        "g_pre_mix": 1.0 + nrm(ks[10], (L, D_MODEL), 0.05),
        "g_post_mix": 1.0 + nrm(ks[11], (L, D_MODEL), 0.05),
        "g_pre_ffn": 1.0 + nrm(ks[12], (L, D_MODEL), 0.05),
        "g_post_ffn": 1.0 + nrm(ks[13], (L, D_MODEL), 0.05),
        "w_in": nrm(ks[14], (L, D_MODEL, D_IN), D_MODEL ** -0.5),
        "conv_w": nrm(ks[15], (L, CONV_W, 1, CONV_DIM), CONV_W ** -0.5),
        "conv_b": nrm(ks[16], (L, CONV_DIM), 0.01),
        "dt_bias": dt0 + jnp.log(-jnp.expm1(-dt0)),
        "a_log": jnp.log(jax.random.uniform(ks[17], (L, H_SSD), f32, 1.0, 16.0)),
        "d_skip": 1.0 + nrm(ks[18], (L, H_SSD), 0.05),
        "g_attn_norm": 1.0 + nrm(ks[19], (L, D_ATT), 0.05),
        "g_ssd_norm": 1.0 + nrm(ks[21], (L, D_SSD), 0.05),
        "w_out": nrm(ks[22], (L, D_ATT + D_SSD, D_MODEL), (D_ATT + D_SSD) ** -0.5),
        "w_pq": nrm(ks[23], (L, D_MODEL, PEER_HEADS * PEER_QDIM), D_MODEL ** -0.5),
        "sub_keys": nrm(ks[24], (L, 2, N_KEYS, PEER_HALF), PEER_HALF ** -0.5),
        "u_tab": nrm(ks[25], (L, N_EXPERTS, D_MODEL), D_MODEL ** -0.5),
        "v_tab": nrm(ks[26], (L, N_EXPERTS, D_MODEL), D_MODEL ** -0.5),
    }


def reference(x_prompt, x_sample, c_prompt, c_sample, cache_k, cache_v, state_ssm, state_conv,
              w_ada, b_ada, g_pre_mix, g_post_mix, g_pre_ffn, g_post_ffn, w_in, conv_w, conv_b,
              dt_bias, a_log, d_skip, g_attn_norm, g_ssd_norm, w_out, w_pq, sub_keys, u_tab, v_tab):
    bp = x_prompt.shape[0]
    xp, xq = x_prompt, x_sample
    kp, vp, sp, cp, kq, vq, sq, cq = [], [], [], [], [], [], [], []
    for i in range(DEPTH):
        lw = (w_ada[i], b_ada[i], g_pre_mix[i], g_post_mix[i], g_pre_ffn[i], g_post_ffn[i],
              w_in[i], conv_w[i], conv_b[i], dt_bias[i], a_log[i], d_skip[i], g_attn_norm[i],
              g_ssd_norm[i], w_out[i], w_pq[i], sub_keys[i], u_tab[i], v_tab[i])
        ssm_zero = jnp.zeros((bp, H_SSD, SSD_HEAD_DIM, SSD_STATE), xp.dtype)
        conv_zero = jnp.zeros((bp, CONV_W - 1, CONV_DIM), xp.dtype)
        xp, k1, v1, s1, c1 = _layer(xp, c_prompt, None, None, ssm_zero, conv_zero, *lw)
        xq, k2, v2, s2, c2 = _layer(xq, c_sample, cache_k[i], cache_v[i], state_ssm[i], state_conv[i], *lw)
        kp.append(k1); vp.append(v1); sp.append(s1); cp.append(c1)
        kq.append(k2); vq.append(v2); sq.append(s2); cq.append(c2)
    return (xp, xq, jnp.stack(kp), jnp.stack(vp), jnp.stack(sp), jnp.stack(cp),
            jnp.stack(kq), jnp.stack(vq), jnp.stack(sq), jnp.stack(cq))
```

```python
import functools

import jax
import jax.numpy as jnp
from jax import lax
from jax.experimental import pallas as pl
from jax.experimental.pallas import tpu as pltpu

F32 = jnp.float32
BF16 = jnp.bfloat16
EPS = 1e-6

H_ATT = 8
HEAD_DIM = 64
D_ATT = H_ATT * HEAD_DIM
H_SSD = 8
SSD_P = 64
D_SSD = H_SSD * SSD_P
SSD_GROUPS = 2
HEADS_PER_GROUP = H_SSD // SSD_GROUPS
SSD_N = 64
CONV_W = 4
CONV_DIM = D_SSD + 2 * SSD_GROUPS * SSD_N
N_KEYS = 128
PEER_HEADS = 8
PEER_TOPK = 16
PEER_HALF = 64

LANES = 128
SUBLANES = 8
ROW_TILE = 128
VMEM_LIMIT = 56 * 1024 * 1024

NT_DIMS = (((1,), (1,)), ((), ()))


def _cparams(sem):
    return pltpu.CompilerParams(dimension_semantics=sem, vmem_limit_bytes=VMEM_LIMIT)


def _split3(a):
    hi = a.astype(BF16)
    r = a - hi.astype(F32)
    mid = r.astype(BF16)
    lo = (r - mid.astype(F32)).astype(BF16)
    return hi, mid, lo


def _dot_f32(a, b):
    a0, a1, a2 = _split3(a)
    b0, b1, b2 = _split3(b)
    d = functools.partial(jnp.dot, preferred_element_type=F32)
    return (d(a0, b0) + (d(a0, b1) + d(a1, b0))
            + (d(a0, b2) + d(a1, b1) + d(a2, b0)))


def _dot_01(m01, x):
    x0, x1, x2 = _split3(x)
    d = functools.partial(jnp.dot, preferred_element_type=F32)
    return d(m01, x0) + d(m01, x1) + d(m01, x2)


def _sigmoid(x):
    return 1.0 / (1.0 + jnp.exp(-x))


def _softplus(x):
    return jnp.maximum(x, 0.0) + jnp.log(1.0 + jnp.exp(-jnp.abs(x)))


def _rms(x):
    return x * lax.rsqrt(jnp.mean(x * x, axis=-1, keepdims=True) + EPS)


def _ada_kernel(c_ref, w_ref, b_ref, o_ref):
    c = c_ref[...]
    o_ref[...] = _dot_f32(c * _sigmoid(c), w_ref[...]) + b_ref[...]


def _ada(c, w_ada, b_ada):
    nb, d = c.shape
    n_out = w_ada.shape[1]
    bn = d
    return pl.pallas_call(
        _ada_kernel,
        grid=(n_out // bn,),
        in_specs=[pl.BlockSpec((nb, d), lambda j: (0, 0)),
                  pl.BlockSpec((d, bn), lambda j: (0, j)),
                  pl.BlockSpec((1, bn), lambda j: (0, j))],
        out_specs=pl.BlockSpec((nb, bn), lambda j: (0, j)),
        out_shape=jax.ShapeDtypeStruct((nb, n_out), F32),
        compiler_params=_cparams(("arbitrary",)),
        name="ada",
    )(c, w_ada, b_ada.reshape(1, n_out))


def _inproj_kernel(x_ref, mod_ref, g_ref, w_ref, q_ref, k_ref, v_ref, z_ref, xbc_ref, dt_ref, *, d):
    mod = mod_ref[0]
    sh1 = mod[:, 0:d]
    sc1 = mod[:, d:2 * d]
    h = (_rms(x_ref[...]) * g_ref[...] * (1.0 + sc1) + sh1).astype(BF16)

    def proj(lo, hi):
        return jnp.dot(h, w_ref[:, lo:hi], preferred_element_type=F32)

    o = 0
    q_ref[...] = (proj(o, o + D_ATT) * (HEAD_DIM ** -0.5)).astype(BF16)
    o += D_ATT
    k_ref[...] = proj(o, o + D_ATT)
    o += D_ATT
    v_ref[...] = proj(o, o + D_ATT)
    o += D_ATT
    z_ref[...] = proj(o, o + D_SSD)
    o += D_SSD
    xbc_ref[...] = proj(o, o + CONV_DIM)
    o += CONV_DIM
    dt_ref[...] = proj(o, o + LANES)


def _inproj(xf, mod3, g_pre, w_in_p, seq_len, tm):
    n, d = xf.shape
    tiles_per_seq = seq_len // tm
    row = lambda i: (i, 0)
    full = lambda i: (0, 0)
    widths = (D_ATT, D_ATT, D_ATT, D_SSD, CONV_DIM, LANES)
    dtypes = (BF16, F32, F32, F32, F32, F32)
    return pl.pallas_call(
        functools.partial(_inproj_kernel, d=d),
        grid=(n // tm,),
        in_specs=[pl.BlockSpec((tm, d), row),
                  pl.BlockSpec((1, 1, 6 * d), lambda i: (i // tiles_per_seq, 0, 0)),
                  pl.BlockSpec((1, d), full),
                  pl.BlockSpec(w_in_p.shape, full)],
        out_specs=[pl.BlockSpec((tm, w), row) for w in widths],
        out_shape=[jax.ShapeDtypeStruct((n, w), t) for w, t in zip(widths, dtypes)],
        compiler_params=_cparams(("arbitrary",)),
        name="inproj",
    )(xf, mod3, g_pre, w_in_p)


def _attn_kernel(q_ref, kd_ref, vd_ref, kp_ref, vp_ref, o_ref, kbf, vbf, *, tq, n_prev_static):
    qi = pl.program_id(2)
    lp = kp_ref.shape[0]
    cast_rows = min(lp, 512)

    @pl.when(qi == 0)
    def _():
        def cp(c, _):
            rows = pl.ds(pl.multiple_of(c * cast_rows, cast_rows), cast_rows)
            for h in range(2):
                lanes = slice(h * HEAD_DIM, (h + 1) * HEAD_DIM)
                kbf[h, rows, :] = kp_ref[rows, lanes].astype(BF16)
                vbf[h, rows, :] = vp_ref[rows, lanes].astype(BF16)
            return 0
        lax.fori_loop(0, lp // cast_rows, cp, 0)

    row = lax.broadcasted_iota(jnp.int32, (tq, tq), 0)
    col = lax.broadcasted_iota(jnp.int32, (tq, tq), 1)
    valid = col < row
    kk = lax.broadcasted_iota(jnp.int32, (tq, 2 * tq), 0)
    nn = lax.broadcasted_iota(jnp.int32, (tq, 2 * tq), 1)
    msum = jnp.where(nn >= tq, 1.0, jnp.where(kk > nn, 1.0, 0.0)).astype(BF16)

    def weights(z, carry, masked):
        l = jnp.log(1.0 + jnp.exp(-jnp.abs(z)))
        log_keep = jnp.minimum(-z, 0.0) - l
        log_beta = jnp.minimum(z, 0.0) - l
        if masked:
            log_keep = jnp.where(valid, log_keep, 0.0)
        hi = log_keep.astype(BF16)
        lo = (log_keep - hi.astype(F32)).astype(BF16)
        r = jnp.dot(jnp.concatenate([hi, lo], axis=0), msum, preferred_element_type=F32)
        r = r[:tq] + r[tq:]
        w = jnp.exp(log_beta + r[:, :tq] + carry)
        if masked:
            w = jnp.where(valid, w, 0.0)
        return w.astype(BF16), carry + r[:, tq:]

    n_prev = qi if n_prev_static is None else n_prev_static
    outs = []
    for h in range(2):
        lanes = slice(h * HEAD_DIM, (h + 1) * HEAD_DIM)
        qh = q_ref[:, lanes]
        z = lax.dot_general(qh, kd_ref[:, lanes].astype(BF16), NT_DIMS, preferred_element_type=F32)
        w, carry = weights(z, jnp.zeros((tq, tq), F32), True)
        o = jnp.dot(w, vd_ref[:, lanes].astype(BF16), preferred_element_type=F32)

        def body(jj, st, h=h, qh=qh):
            o, carry = st
            rows = pl.ds(pl.multiple_of((n_prev - 1 - jj) * tq, tq), tq)
            z = lax.dot_general(qh, kbf[h, rows, :], NT_DIMS, preferred_element_type=F32)
            w, carry = weights(z, carry, False)
            return o + jnp.dot(w, vbf[h, rows, :], preferred_element_type=F32), carry

        o, _ = lax.fori_loop(0, n_prev, body, (o, carry))
        outs.append(o)
    o_ref[...] = jnp.concatenate(outs, axis=1)


def _attention(q, k, v, k_prev, v_prev, nb, seq_len, prev_len, prev_is_self):
    n = q.shape[0]
    tq = ROW_TILE
    nq = seq_len // tq
    blk = lambda b, hp, qi: (b * nq + qi, hp)
    prev = lambda b, hp, qi: (b, hp)
    kern = functools.partial(_attn_kernel, tq=tq,
                             n_prev_static=None if prev_is_self else prev_len // tq)
    return pl.pallas_call(
        kern,
        grid=(nb, H_ATT // 2, nq),
        in_specs=[pl.BlockSpec((tq, LANES), blk),
                  pl.BlockSpec((tq, LANES), blk),
                  pl.BlockSpec((tq, LANES), blk),
                  pl.BlockSpec((prev_len, LANES), prev),
                  pl.BlockSpec((prev_len, LANES), prev)],
        out_specs=pl.BlockSpec((tq, LANES), blk),
        out_shape=jax.ShapeDtypeStruct((n, D_ATT), F32),
        scratch_shapes=[pltpu.VMEM((2, prev_len, HEAD_DIM), BF16),
                        pltpu.VMEM((2, prev_len, HEAD_DIM), BF16)],
        compiler_params=_cparams(("arbitrary", "arbitrary", "arbitrary")),
        name="attn",
    )(q, k, v, k_prev, v_prev)


def _ssd_kernel(xbc_ref, dt_ref, z_ref, conv0_ref, ssm0_ref, cw_ref, cb_ref, dtb_ref, alog_ref,
                dsk_ref, g_ref, y_ref, ssm_ref, buf, state, *, q, valid):
    ci = pl.program_id(1)
    hist = SUBLANES

    @pl.when(ci == 0)
    def _():
        buf[0:hist, :] = conv0_ref[0]
        state[...] = ssm0_ref[0]

    buf[hist:hist + q, :] = xbc_ref[...]
    cw = cw_ref[...]
    acc = cb_ref[...] + cw[CONV_W - 1:CONV_W, :] * buf[hist:hist + q, :]
    for s in range(1, CONV_W):
        acc = acc + cw[CONV_W - 1 - s:CONV_W - s, :] * buf[hist - s:hist - s + q, :]
    xc = acc * _sigmoid(acc)
    buf[0:hist, :] = buf[q:q + hist, :]

    lane = lax.broadcasted_iota(jnp.int32, (q, LANES), 1)
    rowi = lax.broadcasted_iota(jnp.int32, (q, LANES), 0)
    dt = _softplus(dt_ref[...] + dtb_ref[...])
    dt = jnp.where(lane < H_SSD, dt, 0.0)
    if valid < q:
        dt = jnp.where(rowi < valid, dt, 0.0)
    d_a = dt * (-jnp.exp(alog_ref[...]))

    tt = lax.broadcasted_iota(jnp.int32, (q, q), 0)
    ss = lax.broadcasted_iota(jnp.int32, (q, q), 1)
    causal = ss <= tt
    cum = _dot_01(jnp.where(causal, 1.0, 0.0).astype(BF16), d_a)
    cum_t = cum.T
    last = cum[valid - 1:valid, :]
    e_cum = jnp.exp(cum)
    e_end = jnp.exp(last - cum)
    e_last = jnp.exp(last)

    ys = []
    for g in range(SSD_GROUPS):
        b_off = D_SSD + g * SSD_N
        c_off = D_SSD + SSD_GROUPS * SSD_N + g * SSD_N
        bg = xc[:, b_off:b_off + SSD_N].astype(BF16)
        cg = xc[:, c_off:c_off + SSD_N].astype(BF16)
        cb = lax.dot_general(cg, bg, NT_DIMS, preferred_element_type=F32)
        for pair in range(HEADS_PER_GROUP // 2):
            xws = []
            for hh in range(2):
                h = g * HEADS_PER_GROUP + pair * 2 + hh
                xh = xc[:, h * SSD_P:(h + 1) * SSD_P]
                seg = cum[:, h:h + 1] - cum_t[h:h + 1, :]
                decay = jnp.exp(jnp.where(causal, seg, -jnp.inf))
                xdt = xh * dt[:, h:h + 1]
                y_diag = jnp.dot((cb * decay).astype(BF16), xdt.astype(BF16),
                                 preferred_element_type=F32)
                st = state[h]
                y_off = lax.dot_general(cg, st.astype(BF16), NT_DIMS,
                                        preferred_element_type=F32) * e_cum[:, h:h + 1]
                ys.append(y_diag + y_off + dsk_ref[:, h * SSD_P:(h + 1) * SSD_P] * xh)
                xws.append(xdt * e_end[:, h:h + 1])
            xw_t = jnp.concatenate(xws, axis=1).T.astype(BF16)
            for hh in range(2):
                h = g * HEADS_PER_GROUP + pair * 2 + hh
                upd = jnp.dot(xw_t[hh * SSD_P:(hh + 1) * SSD_P, :], bg, preferred_element_type=F32)
                state[h] = state[h] * e_last[:, h:h + 1] + upd
    y = jnp.concatenate(ys, axis=1)
    zz = z_ref[...]
    y_ref[...] = _rms(y * (zz * _sigmoid(zz))) * g_ref[...]

    @pl.when(ci == pl.num_programs(1) - 1)
    def _():
        ssm_ref[0] = state[...]


def _ssd(xbc, dt, z, conv0_p, ssm0, cw, cb, dtb, alog, dsk, g_ssd, nb, seq_len, q, valid):
    n = xbc.shape[0]
    nc = seq_len // q
    row = lambda b, c: (b * nc + c, 0)
    full2 = lambda b, c: (0, 0)
    return pl.pallas_call(
        functools.partial(_ssd_kernel, q=q, valid=valid),
        grid=(nb, nc),
        in_specs=[pl.BlockSpec((q, CONV_DIM), row),
                  pl.BlockSpec((q, LANES), row),
                  pl.BlockSpec((q, D_SSD), row),
                  pl.BlockSpec((1, SUBLANES, CONV_DIM), lambda b, c: (b, 0, 0)),
                  pl.BlockSpec((1, H_SSD, SSD_P, SSD_N), lambda b, c: (b, 0, 0, 0)),
                  pl.BlockSpec((CONV_W, CONV_DIM), full2),
                  pl.BlockSpec((1, CONV_DIM), full2),
                  pl.BlockSpec((1, LANES), full2),
                  pl.BlockSpec((1, LANES), full2),
                  pl.BlockSpec((1, D_SSD), full2),
                  pl.BlockSpec((1, D_SSD), full2)],
        out_specs=[pl.BlockSpec((q, D_SSD), row),
                   pl.BlockSpec((1, H_SSD, SSD_P, SSD_N), lambda b, c: (b, 0, 0, 0))],
        out_shape=[jax.ShapeDtypeStruct((n, D_SSD), F32),
                   jax.ShapeDtypeStruct((nb, H_SSD, SSD_P, SSD_N), F32)],
        scratch_shapes=[pltpu.VMEM((q + SUBLANES, CONV_DIM), F32),
                        pltpu.VMEM((H_SSD, SSD_P, SSD_N), F32)],
        compiler_params=_cparams(("arbitrary", "arbitrary")),
        name="ssd",
    )(xbc, dt, z, conv0_p, ssm0, cw, cb, dtb, alog, dsk, g_ssd)


def _top_values(s, count):
    outs = []
    cur = s
    for r in range(count):
        m = jnp.max(cur, axis=0, keepdims=True)
        outs.append(m)
        if r + 1 < count:
            cur = jnp.where(cur == m, -jnp.inf, cur)
    return outs


def _mid_kernel(x_ref, oa_ref, os_ref, mod_ref, ga_ref, gpm_ref, gpf_ref, wo_ref, wpq_ref, sk_ref,
                x1_ref, h2t_ref, at_ref, thr_ref, s2_ref, bt_ref, *, d):
    mod = mod_ref[0]
    gt1 = mod[:, 2 * d:3 * d]
    sh2 = mod[:, 3 * d:4 * d]
    sc2 = mod[:, 4 * d:5 * d]
    attn_o = (_rms(oa_ref[...]) * ga_ref[...]).astype(BF16)
    mix = (jnp.dot(attn_o, wo_ref[0:D_ATT, :], preferred_element_type=F32)
           + jnp.dot(os_ref[...].astype(BF16), wo_ref[D_ATT:D_ATT + D_SSD, :],
                     preferred_element_type=F32))
    x1 = x_ref[...] + gt1 * (_rms(mix) * gpm_ref[...])
    x1_ref[...] = x1
    h2 = _rms(x1) * gpf_ref[...] * (1.0 + sc2) + sh2
    h2t_ref[...] = h2.T.astype(BF16)
    qh = jnp.dot(h2.astype(BF16), wpq_ref[...], preferred_element_type=F32).astype(BF16)
    k1 = sk_ref[0]
    k2 = sk_ref[1]
    for h in range(PEER_HEADS):
        base = h * 2 * PEER_HALF
        s1 = lax.dot_general(k1, qh[:, base:base + PEER_HALF], NT_DIMS,
                             preferred_element_type=F32)
        s2 = lax.dot_general(k2, qh[:, base + PEER_HALF:base + 2 * PEER_HALF], NT_DIMS,
                             preferred_element_type=F32)
        t1 = _top_values(s1, PEER_TOPK)
        t2 = _top_values(s2, PEER_TOPK)
        t1_all = jnp.concatenate(t1, axis=0)
        cand = jnp.concatenate([t1_all + t2[b] for b in range(PEER_TOPK)], axis=0)
        best = _top_values(cand, PEER_TOPK)
        zsum = jnp.ones_like(best[0])
        for r in range(1, PEER_TOPK):
            zsum = zsum + jnp.exp(best[r] - best[0])
        at_ref[h] = jnp.exp(s1 - t1[0]) * (1.0 / zsum)
        thr_ref[h] = best[PEER_TOPK - 1] - s1
        s2_ref[h] = s2
        bt_ref[h] = jnp.exp(s2 - t2[0])


def _mid(xf, o_att, o_ssd, mod3, g_attn, g_post_mix, g_pre_ffn, w_out, w_pq, sub_keys, seq_len, tm):
    n, d = xf.shape
    tiles_per_seq = seq_len // tm
    row = lambda i: (i, 0)
    full = lambda i: (0, 0)
    colblk = lambda i: (0, 0, i)
    gate_shape = jax.ShapeDtypeStruct((PEER_HEADS, N_KEYS, n), F32)
    gate_spec = pl.BlockSpec((PEER_HEADS, N_KEYS, tm), colblk)
    return pl.pallas_call(
        functools.partial(_mid_kernel, d=d),
        grid=(n // tm,),
        in_specs=[pl.BlockSpec((tm, d), row),
                  pl.BlockSpec((tm, D_ATT), row),
                  pl.BlockSpec((tm, D_SSD), row),
                  pl.BlockSpec((1, 1, 6 * d), lambda i: (i // tiles_per_seq, 0, 0)),
                  pl.BlockSpec((1, D_ATT), full),
                  pl.BlockSpec((1, d), full),
                  pl.BlockSpec((1, d), full),
                  pl.BlockSpec(w_out.shape, full),
                  pl.BlockSpec(w_pq.shape, full),
                  pl.BlockSpec(sub_keys.shape, lambda i: (0, 0, 0))],
        out_specs=[pl.BlockSpec((tm, d), row),
                   pl.BlockSpec((d, tm), lambda i: (0, i)),
                   gate_spec, gate_spec, gate_spec, gate_spec],
        out_shape=[jax.ShapeDtypeStruct((n, d), F32),
                   jax.ShapeDtypeStruct((d, n), BF16),
                   gate_shape, gate_shape, gate_shape, gate_shape],
        compiler_params=_cparams(("arbitrary",)),
        name="mid",
    )(xf, o_att, o_ssd, mod3, g_attn, g_post_mix, g_pre_ffn, w_out, w_pq, sub_keys)


def _gelu_tanh(x):
    return 0.5 * x * (1.0 + jnp.tanh(0.7978845608028654 * (x + 0.044715 * (x * x * x))))


def _peer_kernel(h2t_ref, u_ref, vt_ref, at_ref, thr_ref, s2_ref, bt_ref, x1_ref, mod_ref, g_ref,
                 o_ref, acc, act, actg, *, d, tm, te, seq_rows):
    j = pl.program_id(1)

    @pl.when(j == 0)
    def _():
        acc[...] = jnp.zeros_like(acc)

    act[...] = _gelu_tanh(jnp.dot(u_ref[...], h2t_ref[...], preferred_element_type=F32))

    for il in range(te // N_KEYS):
        rows = slice(il * N_KEYS, (il + 1) * N_KEYS)
        for lg in range(tm // LANES):
            lanes = slice(lg * LANES, (lg + 1) * LANES)
            gsum = jnp.zeros((N_KEYS, LANES), F32)
            for h in range(PEER_HEADS):
                thr = thr_ref[h, il:il + 1, lanes]
                a = at_ref[h, il:il + 1, lanes]
                gsum = gsum + jnp.where(s2_ref[h, :, lanes] >= thr, bt_ref[h, :, lanes] * a, 0.0)
            actg[rows, lanes] = (act[rows, lanes] * gsum).astype(BF16)
    acc[...] += jnp.dot(vt_ref[...], actg[...], preferred_element_type=F32)

    @pl.when(j == pl.num_programs(1) - 1)
    def _():
        f = _rms(acc[...].T) * g_ref[...]
        for s in range(tm // seq_rows):
            rows = slice(s * seq_rows, (s + 1) * seq_rows)
            gt2 = mod_ref[s][:, 5 * d:6 * d]
            o_ref[rows, :] = x1_ref[rows, :] + gt2 * f[rows, :]


def _peer(h2t, u_bf, vt_bf, at, thr, s2, bt, x1, mod3, g_post_ffn, seq_len, tm, te):
    d, n = h2t.shape
    n_exp = u_bf.shape[0]
    seq_rows = min(seq_len, tm)
    seqs_per_tile = tm // seq_rows
    gate_spec = pl.BlockSpec((PEER_HEADS, N_KEYS, tm), lambda i, j: (0, 0, i))
    irow_spec = pl.BlockSpec((PEER_HEADS, te // N_KEYS, tm), lambda i, j: (0, j, i))
    assert (te // N_KEYS) % SUBLANES == 0
    if seqs_per_tile == 1:
        tiles_per_seq = seq_len // tm
        mod_spec = pl.BlockSpec((1, 1, 6 * d), lambda i, j: (i // tiles_per_seq, 0, 0))
    else:
        mod_spec = pl.BlockSpec((seqs_per_tile, 1, 6 * d), lambda i, j: (i, 0, 0))
    return pl.pallas_call(
        functools.partial(_peer_kernel, d=d, tm=tm, te=te, seq_rows=seq_rows),
        grid=(n // tm, n_exp // te),
        in_specs=[pl.BlockSpec((d, tm), lambda i, j: (0, i)),
                  pl.BlockSpec((te, d), lambda i, j: (j, 0)),
                  pl.BlockSpec((d, te), lambda i, j: (0, j)),
                  irow_spec, irow_spec, gate_spec, gate_spec,
                  pl.BlockSpec((tm, d), lambda i, j: (i, 0)),
                  mod_spec,
                  pl.BlockSpec((1, d), lambda i, j: (0, 0))],
        out_specs=pl.BlockSpec((tm, d), lambda i, j: (i, 0)),
        out_shape=jax.ShapeDtypeStruct((n, d), F32),
        scratch_shapes=[pltpu.VMEM((d, tm), F32),
                        pltpu.VMEM((te, tm), F32),
                        pltpu.VMEM((te, tm), BF16)],
        compiler_params=_cparams(("arbitrary", "arbitrary")),
        name="peer",
    )(h2t, u_bf, vt_bf, at, thr, s2, bt, x1, mod3, g_post_ffn)


def _pick_tile(seq_len, target):
    t = min(seq_len, target)
    assert seq_len % t == 0 and t % ROW_TILE == 0
    return t


def _layer(x, mod, k_past, v_past, ssm0, conv0, valid, wts):
    nb, seq_len, d = x.shape
    n = nb * seq_len
    xf = x.reshape(n, d)
    mod3 = mod.reshape(nb, 1, 6 * d)

    q, k, v, z, xbc, dt = _inproj(xf, mod3, wts["g_pre_mix"], wts["w_in"], seq_len,
                                  _pick_tile(seq_len, 512))
    if k_past is None:
        o_att = _attention(q, k, v, k, v, nb, seq_len, seq_len, True)
    else:
        past = k_past.shape[1]
        o_att = _attention(q, k, v, k_past.reshape(nb * past, D_ATT), v_past.reshape(nb * past, D_ATT),
                           nb, seq_len, past, False)

    conv0_p = jnp.pad(conv0, ((0, 0), (SUBLANES - (CONV_W - 1), 0), (0, 0)))
    chunk = _pick_tile(seq_len, 128)
    assert valid == seq_len or seq_len == chunk
    o_ssd, ssm_new = _ssd(xbc, dt, z, conv0_p, ssm0, wts["conv_w"], wts["conv_b"], wts["dt_bias"],
                          wts["a_log"], wts["d_skip"], wts["g_ssd_norm"], nb, seq_len, chunk,
                          min(valid, chunk))

    x1, h2t, at, thr, s2, bt = _mid(xf, o_att, o_ssd, mod3, wts["g_attn_norm"], wts["g_post_mix"],
                                    wts["g_pre_ffn"], wts["w_out"], wts["w_pq"], wts["sub_keys"],
                                    seq_len, _pick_tile(seq_len, 256))
    tm = 512 if n % 512 == 0 and (seq_len % 512 == 0 or 512 % seq_len == 0) else seq_len
    y = _peer(h2t, wts["u_tab"], wts["vt_tab"], at, thr, s2, bt, x1, mod3, wts["g_post_ffn"],
              seq_len, tm, 1024)

    assert valid >= CONV_W - 1
    conv_new = xbc.reshape(nb, seq_len, CONV_DIM)[:, valid - (CONV_W - 1):valid]
    return (y.reshape(nb, seq_len, d)[:, :valid],
            k.reshape(nb, seq_len, H_ATT, HEAD_DIM)[:, :valid],
            v.reshape(nb, seq_len, H_ATT, HEAD_DIM)[:, :valid],
            ssm_new, conv_new)


def _pad_lanes(a, width):
    return jnp.pad(a, ((0, 0), (0, width - a.shape[1])))


def kernel(x_prompt, x_sample, c_prompt, c_sample, cache_k, cache_v, state_ssm, state_conv, w_ada, b_ada, g_pre_mix, g_post_mix, g_pre_ffn, g_post_ffn, w_in, conv_w, conv_b, dt_bias, a_log, d_skip, g_attn_norm, g_ssd_norm, w_out, w_pq, sub_keys, u_tab, v_tab):
    depth = w_ada.shape[0]
    bp, _, d = x_prompt.shape
    bs, dec_len, _ = x_sample.shape
    d_in = w_in.shape[2]
    d_in_p = -(-d_in // LANES) * LANES
    pad_len = -(-dec_len // ROW_TILE) * ROW_TILE

    xp = x_prompt
    xq = jnp.pad(x_sample, ((0, 0), (0, pad_len - dec_len), (0, 0)))
    outs = [[] for _ in range(8)]
    for i in range(depth):
        wts = dict(
            g_pre_mix=g_pre_mix[i][None], g_post_mix=g_post_mix[i][None],
            g_pre_ffn=g_pre_ffn[i][None], g_post_ffn=g_post_ffn[i][None],
            w_in=_pad_lanes(w_in[i], d_in_p).astype(BF16),
            conv_w=conv_w[i].reshape(CONV_W, CONV_DIM), conv_b=conv_b[i][None],
            dt_bias=_pad_lanes(dt_bias[i][None], LANES), a_log=_pad_lanes(a_log[i][None], LANES),
            d_skip=jnp.repeat(d_skip[i], SSD_P)[None],
            g_attn_norm=g_attn_norm[i][None], g_ssd_norm=g_ssd_norm[i][None],
            w_out=w_out[i].astype(BF16), w_pq=w_pq[i].astype(BF16),
            sub_keys=sub_keys[i].astype(BF16),
            u_tab=u_tab[i].astype(BF16), vt_tab=v_tab[i].T.astype(BF16),
        )
        mod = _ada(jnp.concatenate([c_prompt, c_sample], axis=0), w_ada[i], b_ada[i])
        ssm_zero = jnp.zeros((bp, H_SSD, SSD_P, SSD_N), F32)
        conv_zero = jnp.zeros((bp, CONV_W - 1, CONV_DIM), F32)
        xp, k1, v1, s1, c1 = _layer(xp, mod[:bp], None, None, ssm_zero, conv_zero,
                                    xp.shape[1], wts)
        xq_full, k2, v2, s2, c2 = _layer(xq, mod[bp:], cache_k[i], cache_v[i], state_ssm[i],
                                         state_conv[i], dec_len, wts)
        xq = jnp.pad(xq_full, ((0, 0), (0, pad_len - dec_len), (0, 0))) if i + 1 < depth else xq_full
        for lst, val in zip(outs, (k1, v1, s1, c1, k2, v2, s2, c2)):
            lst.append(val)
    return (xp, xq) + tuple(jnp.stack(o) for o in outs)
```

```python
import functools

import jax
import jax.numpy as jnp
from jax import lax
from jax.experimental import pallas as pl
from jax.experimental.pallas import tpu as pltpu

F32 = jnp.float32
BF16 = jnp.bfloat16
EPS = 1e-6

H_ATT = 8
HEAD_DIM = 64
D_ATT = H_ATT * HEAD_DIM
H_SSD = 8
SSD_P = 64
D_SSD = H_SSD * SSD_P
SSD_GROUPS = 2
HEADS_PER_GROUP = H_SSD // SSD_GROUPS
SSD_N = 64
CONV_W = 4
CONV_DIM = D_SSD + 2 * SSD_GROUPS * SSD_N
N_KEYS = 128
PEER_HEADS = 8
PEER_TOPK = 16
PEER_HALF = 64

LANES = 128
SUBLANES = 8
ROW_TILE = 128
VMEM_LIMIT = 56 * 1024 * 1024

NT_DIMS = (((1,), (1,)), ((), ()))
LOG2E = 1.4426950408889634
MASKED_SCORE = -1e30
ATTN_TQ = 1024
ATTN_TK = 256


def _cparams(sem):
    return pltpu.CompilerParams(dimension_semantics=sem, vmem_limit_bytes=VMEM_LIMIT)


def _split3(a):
    hi = a.astype(BF16)
    r = a - hi.astype(F32)
    mid = r.astype(BF16)
    lo = (r - mid.astype(F32)).astype(BF16)
    return hi, mid, lo


def _dot_f32(a, b):
    a0, a1, a2 = _split3(a)
    b0, b1, b2 = _split3(b)
    d = functools.partial(jnp.dot, preferred_element_type=F32)
    return (d(a0, b0) + (d(a0, b1) + d(a1, b0))
            + (d(a0, b2) + d(a1, b1) + d(a2, b0)))


def _dot_01(m01, x):
    x0, x1, x2 = _split3(x)
    d = functools.partial(jnp.dot, preferred_element_type=F32)
    return d(m01, x0) + d(m01, x1) + d(m01, x2)


def _sigmoid(x):
    return 1.0 / (1.0 + jnp.exp(-x))


def _softplus(x):
    return jnp.maximum(x, 0.0) + jnp.log(1.0 + jnp.exp(-jnp.abs(x)))


def _rms(x):
    return x * lax.rsqrt(jnp.mean(x * x, axis=-1, keepdims=True) + EPS)


def _ada_kernel(c_ref, w_ref, b_ref, o_ref):
    c = c_ref[...]
    o_ref[...] = _dot_f32(c * _sigmoid(c), w_ref[...]) + b_ref[...]


def _ada(c, w_ada, b_ada):
    nb, d = c.shape
    n_out = w_ada.shape[1]
    bn = d
    return pl.pallas_call(
        _ada_kernel,
        grid=(n_out // bn,),
        in_specs=[pl.BlockSpec((nb, d), lambda j: (0, 0)),
                  pl.BlockSpec((d, bn), lambda j: (0, j)),
                  pl.BlockSpec((1, bn), lambda j: (0, j))],
        out_specs=pl.BlockSpec((nb, bn), lambda j: (0, j)),
        out_shape=jax.ShapeDtypeStruct((nb, n_out), F32),
        compiler_params=_cparams(("arbitrary",)),
        name="ada",
    )(c, w_ada, b_ada.reshape(1, n_out))


def _inproj_kernel(x_ref, mod_ref, g_ref, w_ref, q_ref, k_ref, v_ref, z_ref, xbc_ref, dt_ref, *, d):
    mod = mod_ref[0]
    sh1 = mod[:, 0:d]
    sc1 = mod[:, d:2 * d]
    h = (_rms(x_ref[...]) * g_ref[...] * (1.0 + sc1) + sh1).astype(BF16)

    def proj(lo, hi):
        return jnp.dot(h, w_ref[:, lo:hi], preferred_element_type=F32)

    o = 0
    q_ref[...] = (proj(o, o + D_ATT) * (HEAD_DIM ** -0.5 * LOG2E)).astype(BF16)
    o += D_ATT
    k_ref[...] = proj(o, o + D_ATT)
    o += D_ATT
    v_ref[...] = proj(o, o + D_ATT)
    o += D_ATT
    z_ref[...] = proj(o, o + D_SSD)
    o += D_SSD
    xbc_ref[...] = proj(o, o + CONV_DIM)
    o += CONV_DIM
    dt_ref[...] = proj(o, o + LANES)


def _inproj(xf, mod3, g_pre, w_in_p, seq_len, tm):
    n, d = xf.shape
    tiles_per_seq = seq_len // tm
    row = lambda i: (i, 0)
    full = lambda i: (0, 0)
    widths = (D_ATT, D_ATT, D_ATT, D_SSD, CONV_DIM, LANES)
    dtypes = (BF16, F32, F32, F32, F32, F32)
    return pl.pallas_call(
        functools.partial(_inproj_kernel, d=d),
        grid=(n // tm,),
        in_specs=[pl.BlockSpec((tm, d), row),
                  pl.BlockSpec((1, 1, 6 * d), lambda i: (i // tiles_per_seq, 0, 0)),
                  pl.BlockSpec((1, d), full),
                  pl.BlockSpec(w_in_p.shape, full)],
        out_specs=[pl.BlockSpec((tm, w), row) for w in widths],
        out_shape=[jax.ShapeDtypeStruct((n, w), t) for w, t in zip(widths, dtypes)],
        compiler_params=_cparams(("arbitrary",)),
        name="inproj",
    )(xf, mod3, g_pre, w_in_p)


def _attn_kernel(q_ref, kd_ref, vd_ref, kp_ref, vp_ref, o_ref, kbf, vbf, oacc, carry, *,
                 tq, tk, n_prev_static):
    qi = pl.program_id(2)
    lp = kp_ref.shape[0]
    cast_rows = min(lp, 512)

    @pl.when(qi == 0)
    def _():
        def cp(c, _):
            rows = pl.ds(pl.multiple_of(c * cast_rows, cast_rows), cast_rows)
            for h in range(2):
                lanes = slice(h * HEAD_DIM, (h + 1) * HEAD_DIM)
                kbf[h, rows, :] = kp_ref[rows, lanes].astype(BF16)
                vbf[h, rows, :] = vp_ref[rows, lanes].astype(BF16)
            return 0
        lax.fori_loop(0, lp // cast_rows, cp, 0)

    td = min(tq, tk)
    kk = lax.broadcasted_iota(jnp.int32, (tk, tk), 0)
    nn = lax.broadcasted_iota(jnp.int32, (tk, tk), 1)
    msum = jnp.where(kk > nn, 1.0, 0.0).astype(BF16)

    oacc[...] = jnp.zeros_like(oacc)
    carry[...] = jnp.zeros_like(carry)

    def step(h, rows, z2, v):
        nk = z2.shape[1]
        zb = z2.astype(BF16)
        l2 = jnp.log(1.0 + jnp.exp2(-jnp.abs(zb))) * LOG2E
        soft = jnp.maximum(zb, 0.0) + l2
        log_beta = jnp.minimum(zb, 0.0) - l2
        newer = jnp.dot(soft, msum[:nk, :nk], preferred_element_type=F32)
        c = carry[h, rows, :]
        gone = (newer + jnp.concatenate([c] * (nk // LANES), axis=1)).astype(BF16)
        w = jnp.exp2(log_beta - gone)
        carry[h, rows, :] = c + (newer[:, 0:1] + soft[:, 0:1].astype(F32))
        oacc[h, rows, :] += jnp.dot(w, v, preferred_element_type=F32)

    for jb in reversed(range(tq // td)):
        r0 = jb * td
        rows = slice(r0, tq)
        keys = slice(r0, r0 + td)
        rl = lax.broadcasted_iota(jnp.int32, (tq - r0, td), 0)
        cl = lax.broadcasted_iota(jnp.int32, (tq - r0, td), 1)
        causal = cl < rl
        for h in range(2):
            lanes = slice(h * HEAD_DIM, (h + 1) * HEAD_DIM)
            z2 = lax.dot_general(q_ref[rows, lanes], kd_ref[keys, lanes].astype(BF16), NT_DIMS,
                                 preferred_element_type=F32)
            step(h, rows, jnp.where(causal, z2, MASKED_SCORE), vd_ref[keys, lanes].astype(BF16))

    n_prev = qi * (tq // tk) if n_prev_static is None else n_prev_static
    unroll = 2 if ((tq // tk) if n_prev_static is None else n_prev_static) % 2 == 0 else 1

    def body(jj, _):
        for u in range(unroll):
            keys = pl.ds(pl.multiple_of((n_prev - 1 - unroll * jj - u) * tk, tk), tk)
            for h in range(2):
                lanes = slice(h * HEAD_DIM, (h + 1) * HEAD_DIM)
                z2 = lax.dot_general(q_ref[:, lanes], kbf[h, keys, :], NT_DIMS,
                                     preferred_element_type=F32)
                step(h, slice(0, tq), z2, vbf[h, keys, :])
        return 0

    lax.fori_loop(0, n_prev // unroll, body, 0)
    o_ref[...] = jnp.concatenate([oacc[0], oacc[1]], axis=1)


def _attention(q, k, v, k_prev, v_prev, nb, seq_len, prev_len, prev_is_self, tq, tk):
    n = q.shape[0]
    nq = seq_len // tq
    assert seq_len % tq == 0 and prev_len % tk == 0 and (tq % tk == 0 or tk % tq == 0)
    blk = lambda b, hp, qi: (b * nq + qi, hp)
    prev = lambda b, hp, qi: (b, hp)
    kern = functools.partial(_attn_kernel, tq=tq, tk=tk,
                             n_prev_static=None if prev_is_self else prev_len // tk)
    return pl.pallas_call(
        kern,
        grid=(nb, H_ATT // 2, nq),
        in_specs=[pl.BlockSpec((tq, LANES), blk),
                  pl.BlockSpec((tq, LANES), blk),
                  pl.BlockSpec((tq, LANES), blk),
                  pl.BlockSpec((prev_len, LANES), prev),
                  pl.BlockSpec((prev_len, LANES), prev)],
        out_specs=pl.BlockSpec((tq, LANES), blk),
        out_shape=jax.ShapeDtypeStruct((n, D_ATT), F32),
        scratch_shapes=[pltpu.VMEM((2, prev_len, HEAD_DIM), BF16),
                        pltpu.VMEM((2, prev_len, HEAD_DIM), BF16),
                        pltpu.VMEM((2, tq, HEAD_DIM), F32),
                        pltpu.VMEM((2, tq, LANES), F32)],
        compiler_params=_cparams(("arbitrary", "arbitrary", "arbitrary")),
        name="attn",
    )(q, k, v, k_prev, v_prev)


def _ssd_kernel(xbc_ref, dt_ref, z_ref, conv0_ref, ssm0_ref, cw_ref, cb_ref, dtb_ref, alog_ref,
                dsk_ref, g_ref, y_ref, ssm_ref, buf, state, *, q, valid):
    ci = pl.program_id(1)
    hist = SUBLANES

    @pl.when(ci == 0)
    def _():
        buf[0:hist, :] = conv0_ref[0]
        state[...] = ssm0_ref[0]

    buf[hist:hist + q, :] = xbc_ref[...]
    cw = cw_ref[...]
    acc = cb_ref[...] + cw[CONV_W - 1:CONV_W, :] * buf[hist:hist + q, :]
    for s in range(1, CONV_W):
        acc = acc + cw[CONV_W - 1 - s:CONV_W - s, :] * buf[hist - s:hist - s + q, :]
    xc = acc * _sigmoid(acc)
    buf[0:hist, :] = buf[q:q + hist, :]

    lane = lax.broadcasted_iota(jnp.int32, (q, LANES), 1)
    rowi = lax.broadcasted_iota(jnp.int32, (q, LANES), 0)
    dt = _softplus(dt_ref[...] + dtb_ref[...])
    dt = jnp.where(lane < H_SSD, dt, 0.0)
    if valid < q:
        dt = jnp.where(rowi < valid, dt, 0.0)
    d_a = dt * (-jnp.exp(alog_ref[...]))

    tt = lax.broadcasted_iota(jnp.int32, (q, q), 0)
    ss = lax.broadcasted_iota(jnp.int32, (q, q), 1)
    causal = ss <= tt
    cum = _dot_01(jnp.where(causal, 1.0, 0.0).astype(BF16), d_a)
    cum_t = cum.T
    last = cum[valid - 1:valid, :]
    e_cum = jnp.exp(cum)
    e_end = jnp.exp(last - cum)
    e_last = jnp.exp(last)

    ys = []
    for g in range(SSD_GROUPS):
        b_off = D_SSD + g * SSD_N
        c_off = D_SSD + SSD_GROUPS * SSD_N + g * SSD_N
        bg = xc[:, b_off:b_off + SSD_N].astype(BF16)
        cg = xc[:, c_off:c_off + SSD_N].astype(BF16)
        cb = lax.dot_general(cg, bg, NT_DIMS, preferred_element_type=F32)
        for pair in range(HEADS_PER_GROUP // 2):
            xws = []
            for hh in range(2):
                h = g * HEADS_PER_GROUP + pair * 2 + hh
                xh = xc[:, h * SSD_P:(h + 1) * SSD_P]
                seg = cum[:, h:h + 1] - cum_t[h:h + 1, :]
                decay = jnp.exp(jnp.where(causal, seg, -jnp.inf))
                xdt = xh * dt[:, h:h + 1]
                y_diag = jnp.dot((cb * decay).astype(BF16), xdt.astype(BF16),
                                 preferred_element_type=F32)
                st = state[h]
                y_off = lax.dot_general(cg, st.astype(BF16), NT_DIMS,
                                        preferred_element_type=F32) * e_cum[:, h:h + 1]
                ys.append(y_diag + y_off + dsk_ref[:, h * SSD_P:(h + 1) * SSD_P] * xh)
                xws.append(xdt * e_end[:, h:h + 1])
            xw_t = jnp.concatenate(xws, axis=1).T.astype(BF16)
            for hh in range(2):
                h = g * HEADS_PER_GROUP + pair * 2 + hh
                upd = jnp.dot(xw_t[hh * SSD_P:(hh + 1) * SSD_P, :], bg, preferred_element_type=F32)
                state[h] = state[h] * e_last[:, h:h + 1] + upd
    y = jnp.concatenate(ys, axis=1)
    zz = z_ref[...]
    y_ref[...] = _rms(y * (zz * _sigmoid(zz))) * g_ref[...]

    @pl.when(ci == pl.num_programs(1) - 1)
    def _():
        ssm_ref[0] = state[...]


def _ssd(xbc, dt, z, conv0_p, ssm0, cw, cb, dtb, alog, dsk, g_ssd, nb, seq_len, q, valid):
    n = xbc.shape[0]
    nc = seq_len // q
    row = lambda b, c: (b * nc + c, 0)
    full2 = lambda b, c: (0, 0)
    return pl.pallas_call(
        functools.partial(_ssd_kernel, q=q, valid=valid),
        grid=(nb, nc),
        in_specs=[pl.BlockSpec((q, CONV_DIM), row),
                  pl.BlockSpec((q, LANES), row),
                  pl.BlockSpec((q, D_SSD), row),
                  pl.BlockSpec((1, SUBLANES, CONV_DIM), lambda b, c: (b, 0, 0)),
                  pl.BlockSpec((1, H_SSD, SSD_P, SSD_N), lambda b, c: (b, 0, 0, 0)),
                  pl.BlockSpec((CONV_W, CONV_DIM), full2),
                  pl.BlockSpec((1, CONV_DIM), full2),
                  pl.BlockSpec((1, LANES), full2),
                  pl.BlockSpec((1, LANES), full2),
                  pl.BlockSpec((1, D_SSD), full2),
                  pl.BlockSpec((1, D_SSD), full2)],
        out_specs=[pl.BlockSpec((q, D_SSD), row),
                   pl.BlockSpec((1, H_SSD, SSD_P, SSD_N), lambda b, c: (b, 0, 0, 0))],
        out_shape=[jax.ShapeDtypeStruct((n, D_SSD), F32),
                   jax.ShapeDtypeStruct((nb, H_SSD, SSD_P, SSD_N), F32)],
        scratch_shapes=[pltpu.VMEM((q + SUBLANES, CONV_DIM), F32),
                        pltpu.VMEM((H_SSD, SSD_P, SSD_N), F32)],
        compiler_params=_cparams(("arbitrary", "arbitrary")),
        name="ssd",
    )(xbc, dt, z, conv0_p, ssm0, cw, cb, dtb, alog, dsk, g_ssd)


def _top_values(s, count):
    outs = []
    cur = s
    for r in range(count):
        m = jnp.max(cur, axis=0, keepdims=True)
        outs.append(m)
        if r + 1 < count:
            cur = jnp.where(cur == m, -jnp.inf, cur)
    return outs


def _mid_kernel(x_ref, oa_ref, os_ref, mod_ref, ga_ref, gpm_ref, gpf_ref, wo_ref, wpq_ref, sk_ref,
                x1_ref, h2t_ref, at_ref, thr_ref, s2_ref, bt_ref, *, d):
    mod = mod_ref[0]
    gt1 = mod[:, 2 * d:3 * d]
    sh2 = mod[:, 3 * d:4 * d]
    sc2 = mod[:, 4 * d:5 * d]
    attn_o = (_rms(oa_ref[...]) * ga_ref[...]).astype(BF16)
    mix = (jnp.dot(attn_o, wo_ref[0:D_ATT, :], preferred_element_type=F32)
           + jnp.dot(os_ref[...].astype(BF16), wo_ref[D_ATT:D_ATT + D_SSD, :],
                     preferred_element_type=F32))
    x1 = x_ref[...] + gt1 * (_rms(mix) * gpm_ref[...])
    x1_ref[...] = x1
    h2 = _rms(x1) * gpf_ref[...] * (1.0 + sc2) + sh2
    h2t_ref[...] = h2.T.astype(BF16)
    qh = jnp.dot(h2.astype(BF16), wpq_ref[...], preferred_element_type=F32).astype(BF16)
    k1 = sk_ref[0]
    k2 = sk_ref[1]
    for h in range(PEER_HEADS):
        base = h * 2 * PEER_HALF
        s1 = lax.dot_general(k1, qh[:, base:base + PEER_HALF], NT_DIMS,
                             preferred_element_type=F32)
        s2 = lax.dot_general(k2, qh[:, base + PEER_HALF:base + 2 * PEER_HALF], NT_DIMS,
                             preferred_element_type=F32)
        t1 = _top_values(s1, PEER_TOPK)
        t2 = _top_values(s2, PEER_TOPK)
        t1_all = jnp.concatenate(t1, axis=0)
        cand = jnp.concatenate([t1_all + t2[b] for b in range(PEER_TOPK)], axis=0)
        best = _top_values(cand, PEER_TOPK)
        zsum = jnp.ones_like(best[0])
        for r in range(1, PEER_TOPK):
            zsum = zsum + jnp.exp(best[r] - best[0])
        at_ref[h] = jnp.exp(s1 - t1[0]) * (1.0 / zsum)
        thr_ref[h] = best[PEER_TOPK - 1] - s1
        s2_ref[h] = s2
        bt_ref[h] = jnp.exp(s2 - t2[0])


def _mid(xf, o_att, o_ssd, mod3, g_attn, g_post_mix, g_pre_ffn, w_out, w_pq, sub_keys, seq_len, tm):
    n, d = xf.shape
    tiles_per_seq = seq_len // tm
    row = lambda i: (i, 0)
    full = lambda i: (0, 0)
    colblk = lambda i: (0, 0, i)
    gate_shape = jax.ShapeDtypeStruct((PEER_HEADS, N_KEYS, n), F32)
    gate_spec = pl.BlockSpec((PEER_HEADS, N_KEYS, tm), colblk)
    return pl.pallas_call(
        functools.partial(_mid_kernel, d=d),
        grid=(n // tm,),
        in_specs=[pl.BlockSpec((tm, d), row),
                  pl.BlockSpec((tm, D_ATT), row),
                  pl.BlockSpec((tm, D_SSD), row),
                  pl.BlockSpec((1, 1, 6 * d), lambda i: (i // tiles_per_seq, 0, 0)),
                  pl.BlockSpec((1, D_ATT), full),
                  pl.BlockSpec((1, d), full),
                  pl.BlockSpec((1, d), full),
                  pl.BlockSpec(w_out.shape, full),
                  pl.BlockSpec(w_pq.shape, full),
                  pl.BlockSpec(sub_keys.shape, lambda i: (0, 0, 0))],
        out_specs=[pl.BlockSpec((tm, d), row),
                   pl.BlockSpec((d, tm), lambda i: (0, i)),
                   gate_spec, gate_spec, gate_spec, gate_spec],
        out_shape=[jax.ShapeDtypeStruct((n, d), F32),
                   jax.ShapeDtypeStruct((d, n), BF16),
                   gate_shape, gate_shape, gate_shape, gate_shape],
        compiler_params=_cparams(("arbitrary",)),
        name="mid",
    )(xf, o_att, o_ssd, mod3, g_attn, g_post_mix, g_pre_ffn, w_out, w_pq, sub_keys)


def _gelu_tanh(x):
    return 0.5 * x * (1.0 + jnp.tanh(0.7978845608028654 * (x + 0.044715 * (x * x * x))))


def _peer_kernel(h2t_ref, u_ref, vt_ref, at_ref, thr_ref, s2_ref, bt_ref, x1_ref, mod_ref, g_ref,
                 o_ref, acc, act, actg, *, d, tm, te, seq_rows):
    j = pl.program_id(1)

    @pl.when(j == 0)
    def _():
        acc[...] = jnp.zeros_like(acc)

    act[...] = _gelu_tanh(jnp.dot(u_ref[...], h2t_ref[...], preferred_element_type=F32))

    for il in range(te // N_KEYS):
        rows = slice(il * N_KEYS, (il + 1) * N_KEYS)
        for lg in range(tm // LANES):
            lanes = slice(lg * LANES, (lg + 1) * LANES)
            gsum = jnp.zeros((N_KEYS, LANES), F32)
            for h in range(PEER_HEADS):
                thr = thr_ref[h, il:il + 1, lanes]
                a = at_ref[h, il:il + 1, lanes]
                gsum = gsum + jnp.where(s2_ref[h, :, lanes] >= thr, bt_ref[h, :, lanes] * a, 0.0)
            actg[rows, lanes] = (act[rows, lanes] * gsum).astype(BF16)
    acc[...] += jnp.dot(vt_ref[...], actg[...], preferred_element_type=F32)

    @pl.when(j == pl.num_programs(1) - 1)
    def _():
        f = _rms(acc[...].T) * g_ref[...]
        for s in range(tm // seq_rows):
            rows = slice(s * seq_rows, (s + 1) * seq_rows)
            gt2 = mod_ref[s][:, 5 * d:6 * d]
            o_ref[rows, :] = x1_ref[rows, :] + gt2 * f[rows, :]


def _peer(h2t, u_bf, vt_bf, at, thr, s2, bt, x1, mod3, g_post_ffn, seq_len, tm, te):
    d, n = h2t.shape
    n_exp = u_bf.shape[0]
    seq_rows = min(seq_len, tm)
    seqs_per_tile = tm // seq_rows
    gate_spec = pl.BlockSpec((PEER_HEADS, N_KEYS, tm), lambda i, j: (0, 0, i))
    irow_spec = pl.BlockSpec((PEER_HEADS, te // N_KEYS, tm), lambda i, j: (0, j, i))
    assert (te // N_KEYS) % SUBLANES == 0
    if seqs_per_tile == 1:
        tiles_per_seq = seq_len // tm
        mod_spec = pl.BlockSpec((1, 1, 6 * d), lambda i, j: (i // tiles_per_seq, 0, 0))
    else:
        mod_spec = pl.BlockSpec((seqs_per_tile, 1, 6 * d), lambda i, j: (i, 0, 0))
    return pl.pallas_call(
        functools.partial(_peer_kernel, d=d, tm=tm, te=te, seq_rows=seq_rows),
        grid=(n // tm, n_exp // te),
        in_specs=[pl.BlockSpec((d, tm), lambda i, j: (0, i)),
                  pl.BlockSpec((te, d), lambda i, j: (j, 0)),
                  pl.BlockSpec((d, te), lambda i, j: (0, j)),
                  irow_spec, irow_spec, gate_spec, gate_spec,
                  pl.BlockSpec((tm, d), lambda i, j: (i, 0)),
                  mod_spec,
                  pl.BlockSpec((1, d), lambda i, j: (0, 0))],
        out_specs=pl.BlockSpec((tm, d), lambda i, j: (i, 0)),
        out_shape=jax.ShapeDtypeStruct((n, d), F32),
        scratch_shapes=[pltpu.VMEM((d, tm), F32),
                        pltpu.VMEM((te, tm), F32),
                        pltpu.VMEM((te, tm), BF16)],
        compiler_params=_cparams(("arbitrary", "arbitrary")),
        name="peer",
    )(h2t, u_bf, vt_bf, at, thr, s2, bt, x1, mod3, g_post_ffn)


def _pick_tile(seq_len, target):
    t = min(seq_len, target)
    assert seq_len % t == 0 and t % ROW_TILE == 0
    return t


def _layer(x, mod, k_past, v_past, ssm0, conv0, valid, wts):
    nb, seq_len, d = x.shape
    n = nb * seq_len
    xf = x.reshape(n, d)
    mod3 = mod.reshape(nb, 1, 6 * d)

    q, k, v, z, xbc, dt = _inproj(xf, mod3, wts["g_pre_mix"], wts["w_in"], seq_len,
                                  _pick_tile(seq_len, 512))
    tq = _pick_tile(seq_len, ATTN_TQ)
    if k_past is None:
        o_att = _attention(q, k, v, k, v, nb, seq_len, seq_len, True, tq, min(tq, ATTN_TK))
    else:
        past = k_past.shape[1]
        o_att = _attention(q, k, v, k_past.reshape(nb * past, D_ATT), v_past.reshape(nb * past, D_ATT),
                           nb, seq_len, past, False, tq, ATTN_TK)

    conv0_p = jnp.pad(conv0, ((0, 0), (SUBLANES - (CONV_W - 1), 0), (0, 0)))
    chunk = _pick_tile(seq_len, 128)
    assert valid == seq_len or seq_len == chunk
    o_ssd, ssm_new = _ssd(xbc, dt, z, conv0_p, ssm0, wts["conv_w"], wts["conv_b"], wts["dt_bias"],
                          wts["a_log"], wts["d_skip"], wts["g_ssd_norm"], nb, seq_len, chunk,
                          min(valid, chunk))

    x1, h2t, at, thr, s2, bt = _mid(xf, o_att, o_ssd, mod3, wts["g_attn_norm"], wts["g_post_mix"],
                                    wts["g_pre_ffn"], wts["w_out"], wts["w_pq"], wts["sub_keys"],
                                    seq_len, _pick_tile(seq_len, 256))
    tm = 512 if n % 512 == 0 and (seq_len % 512 == 0 or 512 % seq_len == 0) else seq_len
    y = _peer(h2t, wts["u_tab"], wts["vt_tab"], at, thr, s2, bt, x1, mod3, wts["g_post_ffn"],
              seq_len, tm, 1024)

    assert valid >= CONV_W - 1
    conv_new = xbc.reshape(nb, seq_len, CONV_DIM)[:, valid - (CONV_W - 1):valid]
    return (y.reshape(nb, seq_len, d)[:, :valid],
            k.reshape(nb, seq_len, H_ATT, HEAD_DIM)[:, :valid],
            v.reshape(nb, seq_len, H_ATT, HEAD_DIM)[:, :valid],
            ssm_new, conv_new)


def _pad_lanes(a, width):
    return jnp.pad(a, ((0, 0), (0, width - a.shape[1])))


def kernel(x_prompt, x_sample, c_prompt, c_sample, cache_k, cache_v, state_ssm, state_conv, w_ada, b_ada, g_pre_mix, g_post_mix, g_pre_ffn, g_post_ffn, w_in, conv_w, conv_b, dt_bias, a_log, d_skip, g_attn_norm, g_ssd_norm, w_out, w_pq, sub_keys, u_tab, v_tab):
    depth = w_ada.shape[0]
    bp, _, d = x_prompt.shape
    bs, dec_len, _ = x_sample.shape
    d_in = w_in.shape[2]
    d_in_p = -(-d_in // LANES) * LANES
    pad_len = -(-dec_len // ROW_TILE) * ROW_TILE

    xp = x_prompt
    xq = jnp.pad(x_sample, ((0, 0), (0, pad_len - dec_len), (0, 0)))
    outs = [[] for _ in range(8)]
    for i in range(depth):
        wts = dict(
            g_pre_mix=g_pre_mix[i][None], g_post_mix=g_post_mix[i][None],
            g_pre_ffn=g_pre_ffn[i][None], g_post_ffn=g_post_ffn[i][None],
            w_in=_pad_lanes(w_in[i], d_in_p).astype(BF16),
            conv_w=conv_w[i].reshape(CONV_W, CONV_DIM), conv_b=conv_b[i][None],
            dt_bias=_pad_lanes(dt_bias[i][None], LANES), a_log=_pad_lanes(a_log[i][None], LANES),
            d_skip=jnp.repeat(d_skip[i], SSD_P)[None],
            g_attn_norm=g_attn_norm[i][None], g_ssd_norm=g_ssd_norm[i][None],
            w_out=w_out[i].astype(BF16), w_pq=w_pq[i].astype(BF16),
            sub_keys=sub_keys[i].astype(BF16),
            u_tab=u_tab[i].astype(BF16), vt_tab=v_tab[i].T.astype(BF16),
        )
        mod = _ada(jnp.concatenate([c_prompt, c_sample], axis=0), w_ada[i], b_ada[i])
        ssm_zero = jnp.zeros((bp, H_SSD, SSD_P, SSD_N), F32)
        conv_zero = jnp.zeros((bp, CONV_W - 1, CONV_DIM), F32)
        xp, k1, v1, s1, c1 = _layer(xp, mod[:bp], None, None, ssm_zero, conv_zero,
                                    xp.shape[1], wts)
        xq_full, k2, v2, s2, c2 = _layer(xq, mod[bp:], cache_k[i], cache_v[i], state_ssm[i],
                                         state_conv[i], dec_len, wts)
        xq = jnp.pad(xq_full, ((0, 0), (0, pad_len - dec_len), (0, 0))) if i + 1 < depth else xq_full
        for lst, val in zip(outs, (k1, v1, s1, c1, k2, v2, s2, c2)):
            lst.append(val)
    return (xp, xq) + tuple(jnp.stack(o) for o in outs)
```

```python
import functools

import jax
import jax.numpy as jnp
from jax import lax
from jax.experimental import pallas as pl
from jax.experimental.pallas import tpu as pltpu

F32 = jnp.float32
BF16 = jnp.bfloat16
EPS = 1e-6

H_ATT = 8
HEAD_DIM = 64
D_ATT = H_ATT * HEAD_DIM
H_SSD = 8
SSD_P = 64
D_SSD = H_SSD * SSD_P
SSD_GROUPS = 2
HEADS_PER_GROUP = H_SSD // SSD_GROUPS
SSD_N = 64
CONV_W = 4
CONV_DIM = D_SSD + 2 * SSD_GROUPS * SSD_N
N_KEYS = 128
PEER_HEADS = 8
PEER_TOPK = 16
PEER_HALF = 64

LANES = 128
SUBLANES = 8
ROW_TILE = 128
VMEM_LIMIT = 56 * 1024 * 1024

NT_DIMS = (((1,), (1,)), ((), ()))
LOG2E = 1.4426950408889634
MASKED_SCORE = -1e30
ATTN_TQ = 1024
ATTN_TK = 256


def _cparams(sem):
    return pltpu.CompilerParams(dimension_semantics=sem, vmem_limit_bytes=VMEM_LIMIT)


def _split3(a):
    hi = a.astype(BF16)
    r = a - hi.astype(F32)
    mid = r.astype(BF16)
    lo = (r - mid.astype(F32)).astype(BF16)
    return hi, mid, lo


def _dot_f32(a, b):
    a0, a1, a2 = _split3(a)
    b0, b1, b2 = _split3(b)
    d = functools.partial(jnp.dot, preferred_element_type=F32)
    return (d(a0, b0) + (d(a0, b1) + d(a1, b0))
            + (d(a0, b2) + d(a1, b1) + d(a2, b0)))


def _dot_01(m01, x):
    x0, x1, x2 = _split3(x)
    d = functools.partial(jnp.dot, preferred_element_type=F32)
    return d(m01, x0) + d(m01, x1) + d(m01, x2)


def _sigmoid(x):
    return 1.0 / (1.0 + jnp.exp(-x))


def _softplus(x):
    return jnp.maximum(x, 0.0) + jnp.log(1.0 + jnp.exp(-jnp.abs(x)))


def _rms(x):
    return x * lax.rsqrt(jnp.mean(x * x, axis=-1, keepdims=True) + EPS)


def _ada_kernel(c_ref, w_ref, b_ref, o_ref):
    c = c_ref[...]
    o_ref[...] = _dot_f32(c * _sigmoid(c), w_ref[...]) + b_ref[...]


def _ada(c, w_ada, b_ada):
    nb, d = c.shape
    n_out = w_ada.shape[1]
    bn = d
    return pl.pallas_call(
        _ada_kernel,
        grid=(n_out // bn,),
        in_specs=[pl.BlockSpec((nb, d), lambda j: (0, 0)),
                  pl.BlockSpec((d, bn), lambda j: (0, j)),
                  pl.BlockSpec((1, bn), lambda j: (0, j))],
        out_specs=pl.BlockSpec((nb, bn), lambda j: (0, j)),
        out_shape=jax.ShapeDtypeStruct((nb, n_out), F32),
        compiler_params=_cparams(("arbitrary",)),
        name="ada",
    )(c, w_ada, b_ada.reshape(1, n_out))


def _inproj_kernel(x_ref, mod_ref, g_ref, w_ref, q_ref, k_ref, v_ref, z_ref, xbc_ref, dt_ref, *, d):
    mod = mod_ref[0]
    sh1 = mod[:, 0:d]
    sc1 = mod[:, d:2 * d]
    h = (_rms(x_ref[...]) * g_ref[...] * (1.0 + sc1) + sh1).astype(BF16)

    def proj(lo, hi):
        return jnp.dot(h, w_ref[:, lo:hi], preferred_element_type=F32)

    o = 0
    q_ref[...] = (proj(o, o + D_ATT) * (HEAD_DIM ** -0.5 * LOG2E)).astype(BF16)
    o += D_ATT
    k_ref[...] = proj(o, o + D_ATT)
    o += D_ATT
    v_ref[...] = proj(o, o + D_ATT)
    o += D_ATT
    z_ref[...] = proj(o, o + D_SSD)
    o += D_SSD
    xbc_ref[...] = proj(o, o + CONV_DIM)
    o += CONV_DIM
    dt_ref[...] = proj(o, o + LANES)


def _inproj(xf, mod3, g_pre, w_in_p, seq_len, tm):
    n, d = xf.shape
    tiles_per_seq = seq_len // tm
    row = lambda i: (i, 0)
    full = lambda i: (0, 0)
    widths = (D_ATT, D_ATT, D_ATT, D_SSD, CONV_DIM, LANES)
    dtypes = (BF16, F32, F32, F32, F32, F32)
    return pl.pallas_call(
        functools.partial(_inproj_kernel, d=d),
        grid=(n // tm,),
        in_specs=[pl.BlockSpec((tm, d), row),
                  pl.BlockSpec((1, 1, 6 * d), lambda i: (i // tiles_per_seq, 0, 0)),
                  pl.BlockSpec((1, d), full),
                  pl.BlockSpec(w_in_p.shape, full)],
        out_specs=[pl.BlockSpec((tm, w), row) for w in widths],
        out_shape=[jax.ShapeDtypeStruct((n, w), t) for w, t in zip(widths, dtypes)],
        compiler_params=_cparams(("arbitrary",)),
        name="inproj",
    )(xf, mod3, g_pre, w_in_p)


def _attn_kernel(q_ref, kd_ref, vd_ref, kp_ref, vp_ref, o_ref, kbf, vbf, oacc, carry, *,
                 tq, tk, n_prev_static):
    qi = pl.program_id(2)
    lp = kp_ref.shape[0]
    cast_rows = min(lp, 512)

    @pl.when(qi == 0)
    def _():
        def cp(c, _):
            rows = pl.ds(pl.multiple_of(c * cast_rows, cast_rows), cast_rows)
            for h in range(2):
                lanes = slice(h * HEAD_DIM, (h + 1) * HEAD_DIM)
                kbf[h, rows, :] = kp_ref[rows, lanes].astype(BF16)
                vbf[h, rows, :] = vp_ref[rows, lanes].astype(BF16)
            return 0
        lax.fori_loop(0, lp // cast_rows, cp, 0)

    td = min(tq, tk)
    kk = lax.broadcasted_iota(jnp.int32, (tk, tk), 0)
    nn = lax.broadcasted_iota(jnp.int32, (tk, tk), 1)
    msum = jnp.where(kk > nn, 1.0, 0.0).astype(BF16)

    oacc[...] = jnp.zeros_like(oacc)
    carry[...] = jnp.zeros_like(carry)

    def step(h, rows, z2, v):
        nk = z2.shape[1]
        zb = z2.astype(BF16)
        l2 = jnp.log(1.0 + jnp.exp2(-jnp.abs(zb))) * LOG2E
        soft = jnp.maximum(zb, 0.0) + l2
        log_beta = jnp.minimum(zb, 0.0) - l2
        newer = jnp.dot(soft, msum[:nk, :nk], preferred_element_type=F32)
        c = carry[h, rows, :]
        gone = (newer + jnp.concatenate([c] * (nk // LANES), axis=1)).astype(BF16)
        w = jnp.exp2(log_beta - gone)
        carry[h, rows, :] = c + (newer[:, 0:1] + soft[:, 0:1].astype(F32))
        oacc[h, rows, :] += jnp.dot(w, v, preferred_element_type=F32)

    for jb in reversed(range(tq // td)):
        r0 = jb * td
        rows = slice(r0, tq)
        keys = slice(r0, r0 + td)
        rl = lax.broadcasted_iota(jnp.int32, (tq - r0, td), 0)
        cl = lax.broadcasted_iota(jnp.int32, (tq - r0, td), 1)
        causal = cl < rl
        for h in range(2):
            lanes = slice(h * HEAD_DIM, (h + 1) * HEAD_DIM)
            z2 = lax.dot_general(q_ref[rows, lanes], kd_ref[keys, lanes].astype(BF16), NT_DIMS,
                                 preferred_element_type=F32)
            step(h, rows, jnp.where(causal, z2, MASKED_SCORE), vd_ref[keys, lanes].astype(BF16))

    n_prev = qi * (tq // tk) if n_prev_static is None else n_prev_static
    unroll = 2 if ((tq // tk) if n_prev_static is None else n_prev_static) % 2 == 0 else 1

    def body(jj, _):
        for u in range(unroll):
            keys = pl.ds(pl.multiple_of((n_prev - 1 - unroll * jj - u) * tk, tk), tk)
            for h in range(2):
                lanes = slice(h * HEAD_DIM, (h + 1) * HEAD_DIM)
                z2 = lax.dot_general(q_ref[:, lanes], kbf[h, keys, :], NT_DIMS,
                                     preferred_element_type=F32)
                step(h, slice(0, tq), z2, vbf[h, keys, :])
        return 0

    lax.fori_loop(0, n_prev // unroll, body, 0)
    o_ref[...] = jnp.concatenate([oacc[0], oacc[1]], axis=1)


def _attention(q, k, v, k_prev, v_prev, nb, seq_len, prev_len, prev_is_self, tq, tk):
    n = q.shape[0]
    nq = seq_len // tq
    assert seq_len % tq == 0 and prev_len % tk == 0 and (tq % tk == 0 or tk % tq == 0)
    blk = lambda b, hp, qi: (b * nq + qi, hp)
    prev = lambda b, hp, qi: (b, hp)
    kern = functools.partial(_attn_kernel, tq=tq, tk=tk,
                             n_prev_static=None if prev_is_self else prev_len // tk)
    return pl.pallas_call(
        kern,
        grid=(nb, H_ATT // 2, nq),
        in_specs=[pl.BlockSpec((tq, LANES), blk),
                  pl.BlockSpec((tq, LANES), blk),
                  pl.BlockSpec((tq, LANES), blk),
                  pl.BlockSpec((prev_len, LANES), prev),
                  pl.BlockSpec((prev_len, LANES), prev)],
        out_specs=pl.BlockSpec((tq, LANES), blk),
        out_shape=jax.ShapeDtypeStruct((n, D_ATT), F32),
        scratch_shapes=[pltpu.VMEM((2, prev_len, HEAD_DIM), BF16),
                        pltpu.VMEM((2, prev_len, HEAD_DIM), BF16),
                        pltpu.VMEM((2, tq, HEAD_DIM), F32),
                        pltpu.VMEM((2, tq, LANES), F32)],
        compiler_params=_cparams(("arbitrary", "arbitrary", "arbitrary")),
        name="attn",
    )(q, k, v, k_prev, v_prev)


def _ssd_kernel(xbc_ref, dt_ref, z_ref, conv0_ref, ssm0_ref, cw_ref, cb_ref, dtb_ref, alog_ref,
                dsk_ref, g_ref, y_ref, ssm_ref, buf, state, *, q, valid):
    ci = pl.program_id(1)
    hist = SUBLANES

    @pl.when(ci == 0)
    def _():
        buf[0:hist, :] = conv0_ref[0]
        state[...] = ssm0_ref[0]

    buf[hist:hist + q, :] = xbc_ref[...]
    cw = cw_ref[...]
    acc = cb_ref[...] + cw[CONV_W - 1:CONV_W, :] * buf[hist:hist + q, :]
    for s in range(1, CONV_W):
        acc = acc + cw[CONV_W - 1 - s:CONV_W - s, :] * buf[hist - s:hist - s + q, :]
    xc = acc * _sigmoid(acc)
    buf[0:hist, :] = buf[q:q + hist, :]

    lane = lax.broadcasted_iota(jnp.int32, (q, LANES), 1)
    rowi = lax.broadcasted_iota(jnp.int32, (q, LANES), 0)
    dt = _softplus(dt_ref[...] + dtb_ref[...])
    dt = jnp.where(lane < H_SSD, dt, 0.0)
    if valid < q:
        dt = jnp.where(rowi < valid, dt, 0.0)
    d_a = dt * (-jnp.exp(alog_ref[...]))

    tt = lax.broadcasted_iota(jnp.int32, (q, q), 0)
    ss = lax.broadcasted_iota(jnp.int32, (q, q), 1)
    causal = ss <= tt
    cum = _dot_01(jnp.where(causal, 1.0, 0.0).astype(BF16), d_a)
    cum_t = cum.T
    last = cum[valid - 1:valid, :]
    e_cum = jnp.exp(cum)
    e_end = jnp.exp(last - cum)
    e_last = jnp.exp(last)

    ys = []
    for g in range(SSD_GROUPS):
        b_off = D_SSD + g * SSD_N
        c_off = D_SSD + SSD_GROUPS * SSD_N + g * SSD_N
        bg = xc[:, b_off:b_off + SSD_N].astype(BF16)
        cg = xc[:, c_off:c_off + SSD_N].astype(BF16)
        cb = lax.dot_general(cg, bg, NT_DIMS, preferred_element_type=F32)
        for pair in range(HEADS_PER_GROUP // 2):
            xws = []
            for hh in range(2):
                h = g * HEADS_PER_GROUP + pair * 2 + hh
                xh = xc[:, h * SSD_P:(h + 1) * SSD_P]
                seg = cum[:, h:h + 1] - cum_t[h:h + 1, :]
                decay = jnp.exp(jnp.where(causal, seg, -jnp.inf))
                xdt = xh * dt[:, h:h + 1]
                y_diag = jnp.dot((cb * decay).astype(BF16), xdt.astype(BF16),
                                 preferred_element_type=F32)
                st = state[h]
                y_off = lax.dot_general(cg, st.astype(BF16), NT_DIMS,
                                        preferred_element_type=F32) * e_cum[:, h:h + 1]
                ys.append(y_diag + y_off + dsk_ref[:, h * SSD_P:(h + 1) * SSD_P] * xh)
                xws.append(xdt * e_end[:, h:h + 1])
            xw_t = jnp.concatenate(xws, axis=1).T.astype(BF16)
            for hh in range(2):
                h = g * HEADS_PER_GROUP + pair * 2 + hh
                upd = jnp.dot(xw_t[hh * SSD_P:(hh + 1) * SSD_P, :], bg, preferred_element_type=F32)
                state[h] = state[h] * e_last[:, h:h + 1] + upd
    y = jnp.concatenate(ys, axis=1)
    zz = z_ref[...]
    y_ref[...] = _rms(y * (zz * _sigmoid(zz))) * g_ref[...]

    @pl.when(ci == pl.num_programs(1) - 1)
    def _():
        ssm_ref[0] = state[...]


def _ssd(xbc, dt, z, conv0_p, ssm0, cw, cb, dtb, alog, dsk, g_ssd, nb, seq_len, q, valid):
    n = xbc.shape[0]
    nc = seq_len // q
    row = lambda b, c: (b * nc + c, 0)
    full2 = lambda b, c: (0, 0)
    return pl.pallas_call(
        functools.partial(_ssd_kernel, q=q, valid=valid),
        grid=(nb, nc),
        in_specs=[pl.BlockSpec((q, CONV_DIM), row),
                  pl.BlockSpec((q, LANES), row),
                  pl.BlockSpec((q, D_SSD), row),
                  pl.BlockSpec((1, SUBLANES, CONV_DIM), lambda b, c: (b, 0, 0)),
                  pl.BlockSpec((1, H_SSD, SSD_P, SSD_N), lambda b, c: (b, 0, 0, 0)),
                  pl.BlockSpec((CONV_W, CONV_DIM), full2),
                  pl.BlockSpec((1, CONV_DIM), full2),
                  pl.BlockSpec((1, LANES), full2),
                  pl.BlockSpec((1, LANES), full2),
                  pl.BlockSpec((1, D_SSD), full2),
                  pl.BlockSpec((1, D_SSD), full2)],
        out_specs=[pl.BlockSpec((q, D_SSD), row),
                   pl.BlockSpec((1, H_SSD, SSD_P, SSD_N), lambda b, c: (b, 0, 0, 0))],
        out_shape=[jax.ShapeDtypeStruct((n, D_SSD), F32),
                   jax.ShapeDtypeStruct((nb, H_SSD, SSD_P, SSD_N), F32)],
        scratch_shapes=[pltpu.VMEM((q + SUBLANES, CONV_DIM), F32),
                        pltpu.VMEM((H_SSD, SSD_P, SSD_N), F32)],
        compiler_params=_cparams(("arbitrary", "arbitrary")),
        name="ssd",
    )(xbc, dt, z, conv0_p, ssm0, cw, cb, dtb, alog, dsk, g_ssd)


def _top_values(s, count):
    outs = []
    cur = s
    for r in range(count):
        m = jnp.max(cur, axis=0, keepdims=True)
        outs.append(m)
        if r + 1 < count:
            cur = jnp.where(cur == m, -jnp.inf, cur)
    return outs


def _twin_bf16(x):
    bits = pltpu.bitcast(x.astype(BF16).astype(F32), jnp.uint32)
    return bits | (bits >> 16)


def _top_ranked(s, count):
    outs = []
    cur = s
    rank = jnp.full(s.shape, float(count), F32)
    for r in range(count):
        m = jnp.max(cur, axis=0, keepdims=True)
        outs.append(m)
        hit = cur == m
        rank = jnp.where(hit, float(r), rank)
        cur = jnp.where(hit, -jnp.inf, cur)
    return outs, rank


def _mid_kernel(x_ref, oa_ref, os_ref, mod_ref, ga_ref, gpm_ref, gpf_ref, wo_ref, wpq_ref, sk_ref,
                x1_ref, h2t_ref, at_ref, cnt_ref, r2_ref, bt_ref, *, d):
    mod = mod_ref[0]
    gt1 = mod[:, 2 * d:3 * d]
    sh2 = mod[:, 3 * d:4 * d]
    sc2 = mod[:, 4 * d:5 * d]
    attn_o = (_rms(oa_ref[...]) * ga_ref[...]).astype(BF16)
    mix = (jnp.dot(attn_o, wo_ref[0:D_ATT, :], preferred_element_type=F32)
           + jnp.dot(os_ref[...].astype(BF16), wo_ref[D_ATT:D_ATT + D_SSD, :],
                     preferred_element_type=F32))
    x1 = x_ref[...] + gt1 * (_rms(mix) * gpm_ref[...])
    x1_ref[...] = x1
    h2 = _rms(x1) * gpf_ref[...] * (1.0 + sc2) + sh2
    h2t_ref[...] = h2.T.astype(BF16)
    qh = jnp.dot(h2.astype(BF16), wpq_ref[...], preferred_element_type=F32).astype(BF16)
    k1 = sk_ref[0]
    k2 = sk_ref[1]
    for h in range(PEER_HEADS):
        base = h * 2 * PEER_HALF
        s1 = lax.dot_general(k1, qh[:, base:base + PEER_HALF], NT_DIMS,
                             preferred_element_type=F32)
        s2 = lax.dot_general(k2, qh[:, base + PEER_HALF:base + 2 * PEER_HALF], NT_DIMS,
                             preferred_element_type=F32)
        t1, rank1 = _top_ranked(s1, PEER_TOPK)
        t2, rank2 = _top_ranked(s2, PEER_TOPK)
        half = PEER_TOPK // 2
        t1_lo = jnp.concatenate(t1[:half], axis=0)
        t1_hi = jnp.concatenate(t1[half:], axis=0)
        t2_hi = jnp.concatenate(t2[half:], axis=0)
        rank_a = lax.broadcasted_iota(jnp.int32, t1_lo.shape, 0)
        blocks = [t1_lo + t2[0], t1_hi + t2[0], t1[0] + t2_hi]
        for b in range(1, half):
            blocks.append(jnp.where(rank_a < PEER_TOPK // (b + 1), t1_lo + t2[b], -jnp.inf))
        best = _top_values(jnp.concatenate(blocks, axis=0), PEER_TOPK)
        tau = best[PEER_TOPK - 1]
        zsum = jnp.ones_like(best[0])
        for r in range(1, PEER_TOPK):
            zsum = zsum + jnp.exp(best[r] - best[0])
        taken = [jnp.where(blk >= tau, 1.0, 0.0) for blk in blocks]
        cnt_lo = taken[0]
        for blk in taken[3:]:
            cnt_lo = cnt_lo + blk
        a0_extra = jnp.sum(taken[2], axis=0, keepdims=True)
        cnt_lo = cnt_lo + jnp.where(rank_a == 0, a0_extra, 0.0)
        cnt_by_rank = jnp.concatenate([cnt_lo, taken[1]], axis=0)
        cnt = jnp.zeros_like(s1)
        for a in range(PEER_TOPK):
            cnt = jnp.where(rank1 == float(a), cnt_by_rank[a:a + 1, :], cnt)
        at_ref[h] = _twin_bf16(jnp.exp(s1 - t1[0]) * (0.5 / zsum))
        cnt_ref[h] = _twin_bf16(cnt)
        r2_ref[h] = rank2
        bt_ref[h] = jnp.exp(s2 - t2[0])


def _mid(xf, o_att, o_ssd, mod3, g_attn, g_post_mix, g_pre_ffn, w_out, w_pq, sub_keys, seq_len, tm):
    n, d = xf.shape
    tiles_per_seq = seq_len // tm
    row = lambda i: (i, 0)
    full = lambda i: (0, 0)
    colblk = lambda i: (0, 0, i)
    gate_u32 = jax.ShapeDtypeStruct((PEER_HEADS, N_KEYS, n), jnp.uint32)
    gate_f32 = jax.ShapeDtypeStruct((PEER_HEADS, N_KEYS, n), F32)
    gate_spec = pl.BlockSpec((PEER_HEADS, N_KEYS, tm), colblk)
    return pl.pallas_call(
        functools.partial(_mid_kernel, d=d),
        grid=(n // tm,),
        in_specs=[pl.BlockSpec((tm, d), row),
                  pl.BlockSpec((tm, D_ATT), row),
                  pl.BlockSpec((tm, D_SSD), row),
                  pl.BlockSpec((1, 1, 6 * d), lambda i: (i // tiles_per_seq, 0, 0)),
                  pl.BlockSpec((1, D_ATT), full),
                  pl.BlockSpec((1, d), full),
                  pl.BlockSpec((1, d), full),
                  pl.BlockSpec(w_out.shape, full),
                  pl.BlockSpec(w_pq.shape, full),
                  pl.BlockSpec(sub_keys.shape, lambda i: (0, 0, 0))],
        out_specs=[pl.BlockSpec((tm, d), row),
                   pl.BlockSpec((d, tm), lambda i: (0, i)),
                   gate_spec, gate_spec, gate_spec, gate_spec],
        out_shape=[jax.ShapeDtypeStruct((n, d), F32),
                   jax.ShapeDtypeStruct((d, n), BF16),
                   gate_u32, gate_u32, gate_f32, gate_f32],
        compiler_params=_cparams(("arbitrary",)),
        name="mid",
    )(xf, o_att, o_ssd, mod3, g_attn, g_post_mix, g_pre_ffn, w_out, w_pq, sub_keys)


GELU_C = 0.7978845608028654
PEER_SUB = 256
PEER_SUBS_PER_ACC = 2
PEER_TE = 1024


def _twin_rows(word_row, rows):
    packed_rows = 2 * SUBLANES
    tile = pltpu.bitcast(jnp.broadcast_to(word_row, (SUBLANES, word_row.shape[1])), BF16)
    return jnp.concatenate([tile] * (rows // packed_rows), axis=0)


def _gelu_tanh_x2(x):
    return x * (1.0 + jnp.tanh(x * ((x * x) * (GELU_C * 0.044715) + GELU_C)))


def _peer_kernel(h2t_ref, u_ref, vt_ref, at_ref, cnt_ref, r2_ref, bt_ref, x1_ref, mod_ref, g_ref,
                 o_ref, acc, r2s, bts, *, d, tm, te, seq_rows):
    j = pl.program_id(1)

    @pl.when(j == 0)
    def _():
        acc[...] = jnp.zeros_like(acc)
        for h in range(PEER_HEADS):
            r2s[h] = r2_ref[h].astype(BF16)
            bts[h] = bt_ref[h].astype(BF16)

    i_per_sub = PEER_SUB // N_KEYS
    n_sub = te // PEER_SUB

    def scores(s):
        return jnp.dot(u_ref[s * PEER_SUB:(s + 1) * PEER_SUB, :], h2t_ref[...],
                       preferred_element_type=F32)

    sc = scores(0)
    pending = []
    for s in range(n_sub):
        sc_next = scores(s + 1) if s + 1 < n_sub else None
        act = _gelu_tanh_x2(sc.astype(BF16))
        lane_blocks = []
        for lg in range(tm // LANES):
            lanes = slice(lg * LANES, (lg + 1) * LANES)
            gsums = [None] * i_per_sub
            for h in range(PEER_HEADS):
                r2 = r2s[h, :, lanes]
                b = bts[h, :, lanes]
                for il in range(i_per_sub):
                    ii = s * i_per_sub + il
                    cnt = _twin_rows(cnt_ref[h, ii:ii + 1, lanes], N_KEYS)
                    a = _twin_rows(at_ref[h, ii:ii + 1, lanes], N_KEYS)
                    term = jnp.where(r2 < cnt, b * a, 0.0)
                    gsums[il] = term if gsums[il] is None else gsums[il] + term
            lane_blocks.append(jnp.concatenate(
                [act[il * N_KEYS:(il + 1) * N_KEYS, lanes] * gsums[il] for il in range(i_per_sub)],
                axis=0))
        pending.append(jnp.concatenate(lane_blocks, axis=1))
        if len(pending) == PEER_SUBS_PER_ACC or s + 1 == n_sub:
            first = s + 1 - len(pending)
            acc[...] += jnp.dot(vt_ref[:, first * PEER_SUB:(s + 1) * PEER_SUB],
                                jnp.concatenate(pending, axis=0), preferred_element_type=F32)
            pending = []
        sc = sc_next

    @pl.when(j == pl.num_programs(1) - 1)
    def _():
        f = _rms(acc[...].T) * g_ref[...]
        for s in range(tm // seq_rows):
            rows = slice(s * seq_rows, (s + 1) * seq_rows)
            gt2 = mod_ref[s][:, 5 * d:6 * d]
            o_ref[rows, :] = x1_ref[rows, :] + gt2 * f[rows, :]


def _peer(h2t, u_bf, vt_bf, at, cnt, r2, bt, x1, mod3, g_post_ffn, seq_len, tm, te):
    d, n = h2t.shape
    n_exp = u_bf.shape[0]
    seq_rows = min(seq_len, tm)
    seqs_per_tile = tm // seq_rows
    gate_spec = pl.BlockSpec((PEER_HEADS, N_KEYS, tm), lambda i, j: (0, 0, i))
    irow_spec = pl.BlockSpec((PEER_HEADS, te // N_KEYS, tm), lambda i, j: (0, j, i))
    assert (te // N_KEYS) % SUBLANES == 0
    if seqs_per_tile == 1:
        tiles_per_seq = seq_len // tm
        mod_spec = pl.BlockSpec((1, 1, 6 * d), lambda i, j: (i // tiles_per_seq, 0, 0))
    else:
        mod_spec = pl.BlockSpec((seqs_per_tile, 1, 6 * d), lambda i, j: (i, 0, 0))
    return pl.pallas_call(
        functools.partial(_peer_kernel, d=d, tm=tm, te=te, seq_rows=seq_rows),
        grid=(n // tm, n_exp // te),
        in_specs=[pl.BlockSpec((d, tm), lambda i, j: (0, i)),
                  pl.BlockSpec((te, d), lambda i, j: (j, 0)),
                  pl.BlockSpec((d, te), lambda i, j: (0, j)),
                  irow_spec, irow_spec, gate_spec, gate_spec,
                  pl.BlockSpec((tm, d), lambda i, j: (i, 0)),
                  mod_spec,
                  pl.BlockSpec((1, d), lambda i, j: (0, 0))],
        out_specs=pl.BlockSpec((tm, d), lambda i, j: (i, 0)),
        out_shape=jax.ShapeDtypeStruct((n, d), F32),
        scratch_shapes=[pltpu.VMEM((d, tm), F32),
                        pltpu.VMEM((PEER_HEADS, N_KEYS, tm), BF16),
                        pltpu.VMEM((PEER_HEADS, N_KEYS, tm), BF16)],
        compiler_params=_cparams(("arbitrary", "arbitrary")),
        name="peer",
    )(h2t, u_bf, vt_bf, at, cnt, r2, bt, x1, mod3, g_post_ffn)


def _pick_tile(seq_len, target):
    t = min(seq_len, target)
    assert seq_len % t == 0 and t % ROW_TILE == 0
    return t


def _layer(x, mod, k_past, v_past, ssm0, conv0, valid, wts):
    nb, seq_len, d = x.shape
    n = nb * seq_len
    xf = x.reshape(n, d)
    mod3 = mod.reshape(nb, 1, 6 * d)

    q, k, v, z, xbc, dt = _inproj(xf, mod3, wts["g_pre_mix"], wts["w_in"], seq_len,
                                  _pick_tile(seq_len, 512))
    tq = _pick_tile(seq_len, ATTN_TQ)
    if k_past is None:
        o_att = _attention(q, k, v, k, v, nb, seq_len, seq_len, True, tq, min(tq, ATTN_TK))
    else:
        past = k_past.shape[1]
        o_att = _attention(q, k, v, k_past.reshape(nb * past, D_ATT), v_past.reshape(nb * past, D_ATT),
                           nb, seq_len, past, False, tq, ATTN_TK)

    conv0_p = jnp.pad(conv0, ((0, 0), (SUBLANES - (CONV_W - 1), 0), (0, 0)))
    chunk = _pick_tile(seq_len, 128)
    assert valid == seq_len or seq_len == chunk
    o_ssd, ssm_new = _ssd(xbc, dt, z, conv0_p, ssm0, wts["conv_w"], wts["conv_b"], wts["dt_bias"],
                          wts["a_log"], wts["d_skip"], wts["g_ssd_norm"], nb, seq_len, chunk,
                          min(valid, chunk))

    x1, h2t, at, cnt, r2, bt = _mid(xf, o_att, o_ssd, mod3, wts["g_attn_norm"], wts["g_post_mix"],
                                    wts["g_pre_ffn"], wts["w_out"], wts["w_pq"], wts["sub_keys"],
                                    seq_len, _pick_tile(seq_len, 256))
    tm = 512 if n % 512 == 0 and (seq_len % 512 == 0 or 512 % seq_len == 0) else seq_len
    y = _peer(h2t, wts["u_tab"], wts["vt_tab"], at, cnt, r2, bt, x1, mod3, wts["g_post_ffn"],
              seq_len, tm, PEER_TE)

    assert valid >= CONV_W - 1
    conv_new = xbc.reshape(nb, seq_len, CONV_DIM)[:, valid - (CONV_W - 1):valid]
    return (y.reshape(nb, seq_len, d)[:, :valid],
            k.reshape(nb, seq_len, H_ATT, HEAD_DIM)[:, :valid],
            v.reshape(nb, seq_len, H_ATT, HEAD_DIM)[:, :valid],
            ssm_new, conv_new)


def _pad_lanes(a, width):
    return jnp.pad(a, ((0, 0), (0, width - a.shape[1])))


def kernel(x_prompt, x_sample, c_prompt, c_sample, cache_k, cache_v, state_ssm, state_conv, w_ada, b_ada, g_pre_mix, g_post_mix, g_pre_ffn, g_post_ffn, w_in, conv_w, conv_b, dt_bias, a_log, d_skip, g_attn_norm, g_ssd_norm, w_out, w_pq, sub_keys, u_tab, v_tab):
    depth = w_ada.shape[0]
    bp, _, d = x_prompt.shape
    bs, dec_len, _ = x_sample.shape
    d_in = w_in.shape[2]
    d_in_p = -(-d_in // LANES) * LANES
    pad_len = -(-dec_len // ROW_TILE) * ROW_TILE

    xp = x_prompt
    xq = jnp.pad(x_sample, ((0, 0), (0, pad_len - dec_len), (0, 0)))
    outs = [[] for _ in range(8)]
    for i in range(depth):
        wts = dict(
            g_pre_mix=g_pre_mix[i][None], g_post_mix=g_post_mix[i][None],
            g_pre_ffn=g_pre_ffn[i][None], g_post_ffn=g_post_ffn[i][None],
            w_in=_pad_lanes(w_in[i], d_in_p).astype(BF16),
            conv_w=conv_w[i].reshape(CONV_W, CONV_DIM), conv_b=conv_b[i][None],
            dt_bias=_pad_lanes(dt_bias[i][None], LANES), a_log=_pad_lanes(a_log[i][None], LANES),
            d_skip=jnp.repeat(d_skip[i], SSD_P)[None],
            g_attn_norm=g_attn_norm[i][None], g_ssd_norm=g_ssd_norm[i][None],
            w_out=w_out[i].astype(BF16), w_pq=w_pq[i].astype(BF16),
            sub_keys=sub_keys[i].astype(BF16),
            u_tab=u_tab[i].astype(BF16), vt_tab=v_tab[i].T.astype(BF16),
        )
        mod = _ada(jnp.concatenate([c_prompt, c_sample], axis=0), w_ada[i], b_ada[i])
        ssm_zero = jnp.zeros((bp, H_SSD, SSD_P, SSD_N), F32)
        conv_zero = jnp.zeros((bp, CONV_W - 1, CONV_DIM), F32)
        xp, k1, v1, s1, c1 = _layer(xp, mod[:bp], None, None, ssm_zero, conv_zero,
                                    xp.shape[1], wts)
        xq_full, k2, v2, s2, c2 = _layer(xq, mod[bp:], cache_k[i], cache_v[i], state_ssm[i],
                                         state_conv[i], dec_len, wts)
        xq = jnp.pad(xq_full, ((0, 0), (0, pad_len - dec_len), (0, 0))) if i + 1 < depth else xq_full
        for lst, val in zip(outs, (k1, v1, s1, c1, k2, v2, s2, c2)):
            lst.append(val)
    return (xp, xq) + tuple(jnp.stack(o) for o in outs)
```

```python
import functools

import jax
import jax.numpy as jnp
from jax import lax
from jax.experimental import pallas as pl
from jax.experimental.pallas import tpu as pltpu

F32 = jnp.float32
BF16 = jnp.bfloat16
EPS = 1e-6

H_ATT = 8
HEAD_DIM = 64
D_ATT = H_ATT * HEAD_DIM
H_SSD = 8
SSD_P = 64
D_SSD = H_SSD * SSD_P
SSD_GROUPS = 2
HEADS_PER_GROUP = H_SSD // SSD_GROUPS
SSD_N = 64
CONV_W = 4
CONV_DIM = D_SSD + 2 * SSD_GROUPS * SSD_N
N_KEYS = 128
PEER_HEADS = 8
PEER_TOPK = 16
PEER_HALF = 64

LANES = 128
SUBLANES = 8
ROW_TILE = 128
VMEM_LIMIT = 56 * 1024 * 1024

NT_DIMS = (((1,), (1,)), ((), ()))
LOG2E = 1.4426950408889634
MASKED_SCORE = -1e30
ATTN_TQ = 1024
ATTN_TK = 256
ATTN_UNROLL = 4


def _cparams(sem):
    return pltpu.CompilerParams(dimension_semantics=sem, vmem_limit_bytes=VMEM_LIMIT)


def _split3(a):
    hi = a.astype(BF16)
    r = a - hi.astype(F32)
    mid = r.astype(BF16)
    lo = (r - mid.astype(F32)).astype(BF16)
    return hi, mid, lo


def _dot_f32(a, b):
    a0, a1, a2 = _split3(a)
    b0, b1, b2 = _split3(b)
    d = functools.partial(jnp.dot, preferred_element_type=F32)
    return (d(a0, b0) + (d(a0, b1) + d(a1, b0))
            + (d(a0, b2) + d(a1, b1) + d(a2, b0)))


def _dot_01(m01, x):
    x0, x1, x2 = _split3(x)
    d = functools.partial(jnp.dot, preferred_element_type=F32)
    return d(m01, x0) + d(m01, x1) + d(m01, x2)


def _sigmoid(x):
    return 1.0 / (1.0 + jnp.exp(-x))


def _softplus(x):
    return jnp.maximum(x, 0.0) + jnp.log(1.0 + jnp.exp(-jnp.abs(x)))


def _rms(x):
    return x * lax.rsqrt(jnp.mean(x * x, axis=-1, keepdims=True) + EPS)


def _ada_kernel(c_ref, w_ref, b_ref, o_ref):
    c = c_ref[...]
    o_ref[...] = _dot_f32(c * _sigmoid(c), w_ref[...]) + b_ref[...]


def _ada(c, w_ada, b_ada):
    nb, d = c.shape
    n_out = w_ada.shape[1]
    bn = d
    return pl.pallas_call(
        _ada_kernel,
        grid=(n_out // bn,),
        in_specs=[pl.BlockSpec((nb, d), lambda j: (0, 0)),
                  pl.BlockSpec((d, bn), lambda j: (0, j)),
                  pl.BlockSpec((1, bn), lambda j: (0, j))],
        out_specs=pl.BlockSpec((nb, bn), lambda j: (0, j)),
        out_shape=jax.ShapeDtypeStruct((nb, n_out), F32),
        compiler_params=_cparams(("arbitrary",)),
        name="ada",
    )(c, w_ada, b_ada.reshape(1, n_out))


def _inproj_kernel(x_ref, mod_ref, g_ref, w_ref, q_ref, k_ref, v_ref, z_ref, xbc_ref, dt_ref, *, d):
    mod = mod_ref[0]
    sh1 = mod[:, 0:d]
    sc1 = mod[:, d:2 * d]
    h = (_rms(x_ref[...]) * g_ref[...] * (1.0 + sc1) + sh1).astype(BF16)

    def proj(lo, hi):
        return jnp.dot(h, w_ref[:, lo:hi], preferred_element_type=F32)

    o = 0
    q_ref[...] = (proj(o, o + D_ATT) * (HEAD_DIM ** -0.5 * LOG2E)).astype(BF16)
    o += D_ATT
    k_ref[...] = proj(o, o + D_ATT)
    o += D_ATT
    v_ref[...] = proj(o, o + D_ATT)
    o += D_ATT
    z_ref[...] = proj(o, o + D_SSD)
    o += D_SSD
    xbc_ref[...] = proj(o, o + CONV_DIM)
    o += CONV_DIM
    dt_ref[...] = proj(o, o + LANES)


def _inproj(xf, mod3, g_pre, w_in_p, seq_len, tm):
    n, d = xf.shape
    tiles_per_seq = seq_len // tm
    row = lambda i: (i, 0)
    full = lambda i: (0, 0)
    widths = (D_ATT, D_ATT, D_ATT, D_SSD, CONV_DIM, LANES)
    dtypes = (BF16, F32, F32, F32, F32, F32)
    return pl.pallas_call(
        functools.partial(_inproj_kernel, d=d),
        grid=(n // tm,),
        in_specs=[pl.BlockSpec((tm, d), row),
                  pl.BlockSpec((1, 1, 6 * d), lambda i: (i // tiles_per_seq, 0, 0)),
                  pl.BlockSpec((1, d), full),
                  pl.BlockSpec(w_in_p.shape, full)],
        out_specs=[pl.BlockSpec((tm, w), row) for w in widths],
        out_shape=[jax.ShapeDtypeStruct((n, w), t) for w, t in zip(widths, dtypes)],
        compiler_params=_cparams(("arbitrary",)),
        name="inproj",
    )(xf, mod3, g_pre, w_in_p)


def _attn_kernel(q_ref, kd_ref, vd_ref, kp_ref, vp_ref, o_ref, kbf, vbf, oacc, carry, *,
                 tq, tk, n_prev_static):
    qi = pl.program_id(2)
    lp = kp_ref.shape[0]
    cast_rows = min(lp, 512)

    @pl.when(qi == 0)
    def _():
        def cp(c, _):
            rows = pl.ds(pl.multiple_of(c * cast_rows, cast_rows), cast_rows)
            for h in range(2):
                lanes = slice(h * HEAD_DIM, (h + 1) * HEAD_DIM)
                kbf[h, rows, :] = kp_ref[rows, lanes].astype(BF16)
                vbf[h, rows, :] = vp_ref[rows, lanes].astype(BF16)
            return 0
        lax.fori_loop(0, lp // cast_rows, cp, 0)

    td = min(tq, tk)
    kk = lax.broadcasted_iota(jnp.int32, (tk, tk), 0)
    nn = lax.broadcasted_iota(jnp.int32, (tk, tk), 1)
    msum = jnp.where(kk > nn, 1.0, 0.0).astype(BF16)

    oacc[...] = jnp.zeros_like(oacc)
    carry[...] = jnp.zeros_like(carry)

    def step(h, rows, z2, v):
        nk = z2.shape[1]
        zb = z2.astype(BF16)
        l2 = jnp.log(1.0 + jnp.exp2(-jnp.abs(zb))) * LOG2E
        soft = jnp.maximum(zb, 0.0) + l2
        log_beta = jnp.minimum(zb, 0.0) - l2
        newer = jnp.dot(soft, msum[:nk, :nk], preferred_element_type=F32)
        c = carry[h, rows, :]
        gone = (newer + jnp.concatenate([c] * (nk // LANES), axis=1)).astype(BF16)
        w = jnp.exp2(log_beta - gone)
        carry[h, rows, :] = c + (newer[:, 0:1] + soft[:, 0:1].astype(F32))
        oacc[h, rows, :] += jnp.dot(w, v, preferred_element_type=F32)

    for jb in reversed(range(tq // td)):
        r0 = jb * td
        rows = slice(r0, tq)
        keys = slice(r0, r0 + td)
        rl = lax.broadcasted_iota(jnp.int32, (tq - r0, td), 0)
        cl = lax.broadcasted_iota(jnp.int32, (tq - r0, td), 1)
        causal = cl < rl
        for h in range(2):
            lanes = slice(h * HEAD_DIM, (h + 1) * HEAD_DIM)
            z2 = lax.dot_general(q_ref[rows, lanes], kd_ref[keys, lanes].astype(BF16), NT_DIMS,
                                 preferred_element_type=F32)
            step(h, rows, jnp.where(causal, z2, MASKED_SCORE), vd_ref[keys, lanes].astype(BF16))

    n_prev = qi * (tq // tk) if n_prev_static is None else n_prev_static
    prev_multiple = (tq // tk) if n_prev_static is None else n_prev_static
    unroll = next(u for u in (ATTN_UNROLL, 2, 1) if prev_multiple % u == 0)

    def body(jj, _):
        for u in range(unroll):
            keys = pl.ds(pl.multiple_of((n_prev - 1 - unroll * jj - u) * tk, tk), tk)
            for h in range(2):
                lanes = slice(h * HEAD_DIM, (h + 1) * HEAD_DIM)
                z2 = lax.dot_general(q_ref[:, lanes], kbf[h, keys, :], NT_DIMS,
                                     preferred_element_type=F32)
                step(h, slice(0, tq), z2, vbf[h, keys, :])
        return 0

    lax.fori_loop(0, n_prev // unroll, body, 0)
    o_ref[...] = jnp.concatenate([oacc[0], oacc[1]], axis=1)


def _attention(q, k, v, k_prev, v_prev, nb, seq_len, prev_len, prev_is_self, tq, tk):
    n = q.shape[0]
    nq = seq_len // tq
    assert seq_len % tq == 0 and prev_len % tk == 0 and (tq % tk == 0 or tk % tq == 0)
    blk = lambda b, hp, qi: (b * nq + qi, hp)
    prev = lambda b, hp, qi: (b, hp)
    kern = functools.partial(_attn_kernel, tq=tq, tk=tk,
                             n_prev_static=None if prev_is_self else prev_len // tk)
    return pl.pallas_call(
        kern,
        grid=(nb, H_ATT // 2, nq),
        in_specs=[pl.BlockSpec((tq, LANES), blk),
                  pl.BlockSpec((tq, LANES), blk),
                  pl.BlockSpec((tq, LANES), blk),
                  pl.BlockSpec((prev_len, LANES), prev),
                  pl.BlockSpec((prev_len, LANES), prev)],
        out_specs=pl.BlockSpec((tq, LANES), blk),
        out_shape=jax.ShapeDtypeStruct((n, D_ATT), F32),
        scratch_shapes=[pltpu.VMEM((2, prev_len, HEAD_DIM), BF16),
                        pltpu.VMEM((2, prev_len, HEAD_DIM), BF16),
                        pltpu.VMEM((2, tq, HEAD_DIM), F32),
                        pltpu.VMEM((2, tq, LANES), F32)],
        compiler_params=_cparams(("arbitrary", "arbitrary", "arbitrary")),
        name="attn",
    )(q, k, v, k_prev, v_prev)


def _ssd_kernel(xbc_ref, dt_ref, z_ref, conv0_ref, ssm0_ref, cw_ref, cb_ref, dtb_ref, alog_ref,
                dsk_ref, g_ref, y_ref, ssm_ref, buf, state, *, q, valid):
    ci = pl.program_id(1)
    hist = SUBLANES

    @pl.when(ci == 0)
    def _():
        buf[0:hist, :] = conv0_ref[0]
        state[...] = ssm0_ref[0]

    buf[hist:hist + q, :] = xbc_ref[...]
    cw = cw_ref[...]
    acc = cb_ref[...] + cw[CONV_W - 1:CONV_W, :] * buf[hist:hist + q, :]
    for s in range(1, CONV_W):
        acc = acc + cw[CONV_W - 1 - s:CONV_W - s, :] * buf[hist - s:hist - s + q, :]
    xc = acc * _sigmoid(acc)
    buf[0:hist, :] = buf[q:q + hist, :]

    lane = lax.broadcasted_iota(jnp.int32, (q, LANES), 1)
    rowi = lax.broadcasted_iota(jnp.int32, (q, LANES), 0)
    dt = _softplus(dt_ref[...] + dtb_ref[...])
    dt = jnp.where(lane < H_SSD, dt, 0.0)
    if valid < q:
        dt = jnp.where(rowi < valid, dt, 0.0)
    d_a = dt * (-jnp.exp(alog_ref[...]))

    tt = lax.broadcasted_iota(jnp.int32, (q, q), 0)
    ss = lax.broadcasted_iota(jnp.int32, (q, q), 1)
    causal = ss <= tt
    cum = _dot_01(jnp.where(causal, 1.0, 0.0).astype(BF16), d_a)
    cum_t = cum.T
    last = cum[valid - 1:valid, :]
    e_cum = jnp.exp(cum)
    e_end = jnp.exp(last - cum)
    e_last = jnp.exp(last)

    ys = []
    for g in range(SSD_GROUPS):
        b_off = D_SSD + g * SSD_N
        c_off = D_SSD + SSD_GROUPS * SSD_N + g * SSD_N
        bg = xc[:, b_off:b_off + SSD_N].astype(BF16)
        cg = xc[:, c_off:c_off + SSD_N].astype(BF16)
        cb = lax.dot_general(cg, bg, NT_DIMS, preferred_element_type=F32)
        for pair in range(HEADS_PER_GROUP // 2):
            xws = []
            for hh in range(2):
                h = g * HEADS_PER_GROUP + pair * 2 + hh
                xh = xc[:, h * SSD_P:(h + 1) * SSD_P]
                seg = cum[:, h:h + 1] - cum_t[h:h + 1, :]
                decay = jnp.exp(jnp.where(causal, seg, -jnp.inf))
                xdt = xh * dt[:, h:h + 1]
                y_diag = jnp.dot((cb * decay).astype(BF16), xdt.astype(BF16),
                                 preferred_element_type=F32)
                st = state[h]
                y_off = lax.dot_general(cg, st.astype(BF16), NT_DIMS,
                                        preferred_element_type=F32) * e_cum[:, h:h + 1]
                ys.append(y_diag + y_off + dsk_ref[:, h * SSD_P:(h + 1) * SSD_P] * xh)
                xws.append(xdt * e_end[:, h:h + 1])
            xw_t = jnp.concatenate(xws, axis=1).T.astype(BF16)
            for hh in range(2):
                h = g * HEADS_PER_GROUP + pair * 2 + hh
                upd = jnp.dot(xw_t[hh * SSD_P:(hh + 1) * SSD_P, :], bg, preferred_element_type=F32)
                state[h] = state[h] * e_last[:, h:h + 1] + upd
    y = jnp.concatenate(ys, axis=1)
    zz = z_ref[...]
    y_ref[...] = _rms(y * (zz * _sigmoid(zz))) * g_ref[...]

    @pl.when(ci == pl.num_programs(1) - 1)
    def _():
        ssm_ref[0] = state[...]


def _ssd(xbc, dt, z, conv0_p, ssm0, cw, cb, dtb, alog, dsk, g_ssd, nb, seq_len, q, valid):
    n = xbc.shape[0]
    nc = seq_len // q
    row = lambda b, c: (b * nc + c, 0)
    full2 = lambda b, c: (0, 0)
    return pl.pallas_call(
        functools.partial(_ssd_kernel, q=q, valid=valid),
        grid=(nb, nc),
        in_specs=[pl.BlockSpec((q, CONV_DIM), row),
                  pl.BlockSpec((q, LANES), row),
                  pl.BlockSpec((q, D_SSD), row),
                  pl.BlockSpec((1, SUBLANES, CONV_DIM), lambda b, c: (b, 0, 0)),
                  pl.BlockSpec((1, H_SSD, SSD_P, SSD_N), lambda b, c: (b, 0, 0, 0)),
                  pl.BlockSpec((CONV_W, CONV_DIM), full2),
                  pl.BlockSpec((1, CONV_DIM), full2),
                  pl.BlockSpec((1, LANES), full2),
                  pl.BlockSpec((1, LANES), full2),
                  pl.BlockSpec((1, D_SSD), full2),
                  pl.BlockSpec((1, D_SSD), full2)],
        out_specs=[pl.BlockSpec((q, D_SSD), row),
                   pl.BlockSpec((1, H_SSD, SSD_P, SSD_N), lambda b, c: (b, 0, 0, 0))],
        out_shape=[jax.ShapeDtypeStruct((n, D_SSD), F32),
                   jax.ShapeDtypeStruct((nb, H_SSD, SSD_P, SSD_N), F32)],
        scratch_shapes=[pltpu.VMEM((q + SUBLANES, CONV_DIM), F32),
                        pltpu.VMEM((H_SSD, SSD_P, SSD_N), F32)],
        compiler_params=_cparams(("arbitrary", "arbitrary")),
        name="ssd",
    )(xbc, dt, z, conv0_p, ssm0, cw, cb, dtb, alog, dsk, g_ssd)


def _top_values(s, count):
    outs = []
    cur = s
    for r in range(count):
        m = jnp.max(cur, axis=0, keepdims=True)
        outs.append(m)
        if r + 1 < count:
            cur = jnp.where(cur == m, -jnp.inf, cur)
    return outs


def _twin_bf16(x):
    bits = pltpu.bitcast(x.astype(BF16).astype(F32), jnp.uint32)
    return bits | (bits >> 16)


INT32_MIN = -2 ** 31


def _order_key(x):
    bits = pltpu.bitcast(x, jnp.int32)
    return bits ^ ((bits >> 31) & 0x7FFFFFFF)


def _order_key_inverse(key):
    return pltpu.bitcast(key ^ ((key >> 31) & 0x7FFFFFFF), F32)


def _top_ranked(s, count):
    outs = []
    cur = _order_key(s)
    for r in range(count):
        m = jnp.max(cur, axis=0, keepdims=True)
        outs.append(_order_key_inverse(m))
        cur = jnp.where(cur == m, INT32_MIN + r, cur)
    rank = jnp.where(cur < INT32_MIN + count, cur - INT32_MIN, count).astype(F32)
    return outs, rank


def _mid_kernel(x_ref, oa_ref, os_ref, mod_ref, ga_ref, gpm_ref, gpf_ref, wo_ref, wpq_ref, sk_ref,
                x1_ref, h2t_ref, at_ref, cnt_ref, r2_ref, bt_ref, *, d):
    mod = mod_ref[0]
    gt1 = mod[:, 2 * d:3 * d]
    sh2 = mod[:, 3 * d:4 * d]
    sc2 = mod[:, 4 * d:5 * d]
    attn_o = (_rms(oa_ref[...]) * ga_ref[...]).astype(BF16)
    mix = (jnp.dot(attn_o, wo_ref[0:D_ATT, :], preferred_element_type=F32)
           + jnp.dot(os_ref[...].astype(BF16), wo_ref[D_ATT:D_ATT + D_SSD, :],
                     preferred_element_type=F32))
    x1 = x_ref[...] + gt1 * (_rms(mix) * gpm_ref[...])
    x1_ref[...] = x1
    h2 = _rms(x1) * gpf_ref[...] * (1.0 + sc2) + sh2
    h2t_ref[...] = h2.T.astype(BF16)
    qh = jnp.dot(h2.astype(BF16), wpq_ref[...], preferred_element_type=F32).astype(BF16)
    k1 = sk_ref[0]
    k2 = sk_ref[1]
    for h in range(PEER_HEADS):
        base = h * 2 * PEER_HALF
        s1 = lax.dot_general(k1, qh[:, base:base + PEER_HALF], NT_DIMS,
                             preferred_element_type=F32)
        s2 = lax.dot_general(k2, qh[:, base + PEER_HALF:base + 2 * PEER_HALF], NT_DIMS,
                             preferred_element_type=F32)
        t1, rank1 = _top_ranked(s1, PEER_TOPK)
        t2, rank2 = _top_ranked(s2, PEER_TOPK)
        half = PEER_TOPK // 2
        t1_lo = jnp.concatenate(t1[:half], axis=0)
        t1_hi = jnp.concatenate(t1[half:], axis=0)
        t2_hi = jnp.concatenate(t2[half:], axis=0)
        rank_a = lax.broadcasted_iota(jnp.int32, t1_lo.shape, 0)
        blocks = [t1_lo + t2[0], t1_hi + t2[0], t1[0] + t2_hi]
        for b in range(1, half):
            blocks.append(jnp.where(rank_a < PEER_TOPK // (b + 1), t1_lo + t2[b], -jnp.inf))
        best = _top_values(jnp.concatenate(blocks, axis=0), PEER_TOPK)
        tau = best[PEER_TOPK - 1]
        zsum = jnp.ones_like(best[0])
        for r in range(1, PEER_TOPK):
            zsum = zsum + jnp.exp(best[r] - best[0])
        taken = [jnp.where(blk >= tau, 1.0, 0.0) for blk in blocks]
        cnt_lo = taken[0]
        for blk in taken[3:]:
            cnt_lo = cnt_lo + blk
        a0_extra = jnp.sum(taken[2], axis=0, keepdims=True)
        cnt_lo = cnt_lo + jnp.where(rank_a == 0, a0_extra, 0.0)
        cnt_by_rank = jnp.concatenate([cnt_lo, taken[1]], axis=0)
        cnt = jnp.zeros_like(s1)
        for a in range(PEER_TOPK):
            cnt = jnp.where(rank1 == float(a), cnt_by_rank[a:a + 1, :], cnt)
        at_ref[h] = _twin_bf16(jnp.exp(s1 - t1[0]) * (0.5 / zsum))
        cnt_ref[h] = _twin_bf16(cnt)
        r2_ref[h] = rank2
        bt_ref[h] = jnp.exp(s2 - t2[0])


def _mid(xf, o_att, o_ssd, mod3, g_attn, g_post_mix, g_pre_ffn, w_out, w_pq, sub_keys, seq_len, tm):
    n, d = xf.shape
    tiles_per_seq = seq_len // tm
    row = lambda i: (i, 0)
    full = lambda i: (0, 0)
    colblk = lambda i: (0, 0, i)
    gate_u32 = jax.ShapeDtypeStruct((PEER_HEADS, N_KEYS, n), jnp.uint32)
    gate_f32 = jax.ShapeDtypeStruct((PEER_HEADS, N_KEYS, n), F32)
    gate_spec = pl.BlockSpec((PEER_HEADS, N_KEYS, tm), colblk)
    return pl.pallas_call(
        functools.partial(_mid_kernel, d=d),
        grid=(n // tm,),
        in_specs=[pl.BlockSpec((tm, d), row),
                  pl.BlockSpec((tm, D_ATT), row),
                  pl.BlockSpec((tm, D_SSD), row),
                  pl.BlockSpec((1, 1, 6 * d), lambda i: (i // tiles_per_seq, 0, 0)),
                  pl.BlockSpec((1, D_ATT), full),
                  pl.BlockSpec((1, d), full),
                  pl.BlockSpec((1, d), full),
                  pl.BlockSpec(w_out.shape, full),
                  pl.BlockSpec(w_pq.shape, full),
                  pl.BlockSpec(sub_keys.shape, lambda i: (0, 0, 0))],
        out_specs=[pl.BlockSpec((tm, d), row),
                   pl.BlockSpec((d, tm), lambda i: (0, i)),
                   gate_spec, gate_spec, gate_spec, gate_spec],
        out_shape=[jax.ShapeDtypeStruct((n, d), F32),
                   jax.ShapeDtypeStruct((d, n), BF16),
                   gate_u32, gate_u32, gate_f32, gate_f32],
        compiler_params=_cparams(("arbitrary",)),
        name="mid",
    )(xf, o_att, o_ssd, mod3, g_attn, g_post_mix, g_pre_ffn, w_out, w_pq, sub_keys)


GELU_C = 0.7978845608028654
PEER_SUB = 256
PEER_SUBS_PER_ACC = 2
PEER_TE = 1024


def _twin_rows(word_row, rows):
    packed_rows = 2 * SUBLANES
    tile = pltpu.bitcast(jnp.broadcast_to(word_row, (SUBLANES, word_row.shape[1])), BF16)
    return jnp.concatenate([tile] * (rows // packed_rows), axis=0)


def _gelu_tanh_x2(x):
    return x * (1.0 + jnp.tanh(x * ((x * x) * (GELU_C * 0.044715) + GELU_C)))


def _peer_kernel(h2t_ref, u_ref, vt_ref, at_ref, cnt_ref, r2_ref, bt_ref, x1_ref, mod_ref, g_ref,
                 o_ref, acc, r2s, bts, *, d, tm, te, seq_rows):
    j = pl.program_id(1)

    @pl.when(j == 0)
    def _():
        acc[...] = jnp.zeros_like(acc)
        for h in range(PEER_HEADS):
            r2s[h] = r2_ref[h].astype(BF16)
            bts[h] = bt_ref[h].astype(BF16)

    i_per_sub = PEER_SUB // N_KEYS
    n_sub = te // PEER_SUB

    def scores(s):
        return jnp.dot(u_ref[s * PEER_SUB:(s + 1) * PEER_SUB, :], h2t_ref[...],
                       preferred_element_type=F32)

    sc = scores(0)
    pending = []
    for s in range(n_sub):
        sc_next = scores(s + 1) if s + 1 < n_sub else None
        act = _gelu_tanh_x2(sc.astype(BF16))
        lane_blocks = []
        for lg in range(tm // LANES):
            lanes = slice(lg * LANES, (lg + 1) * LANES)
            gsums = [None] * i_per_sub
            for h in range(PEER_HEADS):
                r2 = r2s[h, :, lanes]
                b = bts[h, :, lanes]
                for il in range(i_per_sub):
                    ii = s * i_per_sub + il
                    cnt = _twin_rows(cnt_ref[h, ii:ii + 1, lanes], N_KEYS)
                    a = _twin_rows(at_ref[h, ii:ii + 1, lanes], N_KEYS)
                    term = jnp.clip(cnt - r2, 0.0, 1.0) * (b * a)
                    gsums[il] = term if gsums[il] is None else gsums[il] + term
            lane_blocks.append(jnp.concatenate(
                [act[il * N_KEYS:(il + 1) * N_KEYS, lanes] * gsums[il] for il in range(i_per_sub)],
                axis=0))
        pending.append(jnp.concatenate(lane_blocks, axis=1))
        if len(pending) == PEER_SUBS_PER_ACC or s + 1 == n_sub:
            first = s + 1 - len(pending)
            acc[...] += jnp.dot(vt_ref[:, first * PEER_SUB:(s + 1) * PEER_SUB],
                                jnp.concatenate(pending, axis=0), preferred_element_type=F32)
            pending = []
        sc = sc_next

    @pl.when(j == pl.num_programs(1) - 1)
    def _():
        f = _rms(acc[...].T) * g_ref[...]
        for s in range(tm // seq_rows):
            rows = slice(s * seq_rows, (s + 1) * seq_rows)
            gt2 = mod_ref[s][:, 5 * d:6 * d]
            o_ref[rows, :] = x1_ref[rows, :] + gt2 * f[rows, :]


def _peer(h2t, u_bf, vt_bf, at, cnt, r2, bt, x1, mod3, g_post_ffn, seq_len, tm, te):
    d, n = h2t.shape
    n_exp = u_bf.shape[0]
    seq_rows = min(seq_len, tm)
    seqs_per_tile = tm // seq_rows
    gate_spec = pl.BlockSpec((PEER_HEADS, N_KEYS, tm), lambda i, j: (0, 0, i))
    irow_spec = pl.BlockSpec((PEER_HEADS, te // N_KEYS, tm), lambda i, j: (0, j, i))
    assert (te // N_KEYS) % SUBLANES == 0
    if seqs_per_tile == 1:
        tiles_per_seq = seq_len // tm
        mod_spec = pl.BlockSpec((1, 1, 6 * d), lambda i, j: (i // tiles_per_seq, 0, 0))
    else:
        mod_spec = pl.BlockSpec((seqs_per_tile, 1, 6 * d), lambda i, j: (i, 0, 0))
    return pl.pallas_call(
        functools.partial(_peer_kernel, d=d, tm=tm, te=te, seq_rows=seq_rows),
        grid=(n // tm, n_exp // te),
        in_specs=[pl.BlockSpec((d, tm), lambda i, j: (0, i)),
                  pl.BlockSpec((te, d), lambda i, j: (j, 0)),
                  pl.BlockSpec((d, te), lambda i, j: (0, j)),
                  irow_spec, irow_spec, gate_spec, gate_spec,
                  pl.BlockSpec((tm, d), lambda i, j: (i, 0)),
                  mod_spec,
                  pl.BlockSpec((1, d), lambda i, j: (0, 0))],
        out_specs=pl.BlockSpec((tm, d), lambda i, j: (i, 0)),
        out_shape=jax.ShapeDtypeStruct((n, d), F32),
        scratch_shapes=[pltpu.VMEM((d, tm), F32),
                        pltpu.VMEM((PEER_HEADS, N_KEYS, tm), BF16),
                        pltpu.VMEM((PEER_HEADS, N_KEYS, tm), BF16)],
        compiler_params=_cparams(("arbitrary", "arbitrary")),
        name="peer",
    )(h2t, u_bf, vt_bf, at, cnt, r2, bt, x1, mod3, g_post_ffn)


def _pick_tile(seq_len, target):
    t = min(seq_len, target)
    assert seq_len % t == 0 and t % ROW_TILE == 0
    return t


def _layer(x, mod, k_past, v_past, ssm0, conv0, valid, wts):
    nb, seq_len, d = x.shape
    n = nb * seq_len
    xf = x.reshape(n, d)
    mod3 = mod.reshape(nb, 1, 6 * d)

    q, k, v, z, xbc, dt = _inproj(xf, mod3, wts["g_pre_mix"], wts["w_in"], seq_len,
                                  _pick_tile(seq_len, 512))
    tq = _pick_tile(seq_len, ATTN_TQ)
    if k_past is None:
        o_att = _attention(q, k, v, k, v, nb, seq_len, seq_len, True, tq, min(tq, ATTN_TK))
    else:
        past = k_past.shape[1]
        o_att = _attention(q, k, v, k_past.reshape(nb * past, D_ATT), v_past.reshape(nb * past, D_ATT),
                           nb, seq_len, past, False, tq, ATTN_TK)

    conv0_p = jnp.pad(conv0, ((0, 0), (SUBLANES - (CONV_W - 1), 0), (0, 0)))
    chunk = _pick_tile(seq_len, 128)
    assert valid == seq_len or seq_len == chunk
    o_ssd, ssm_new = _ssd(xbc, dt, z, conv0_p, ssm0, wts["conv_w"], wts["conv_b"], wts["dt_bias"],
                          wts["a_log"], wts["d_skip"], wts["g_ssd_norm"], nb, seq_len, chunk,
                          min(valid, chunk))

    x1, h2t, at, cnt, r2, bt = _mid(xf, o_att, o_ssd, mod3, wts["g_attn_norm"], wts["g_post_mix"],
                                    wts["g_pre_ffn"], wts["w_out"], wts["w_pq"], wts["sub_keys"],
                                    seq_len, _pick_tile(seq_len, 256))
    tm = 512 if n % 512 == 0 and (seq_len % 512 == 0 or 512 % seq_len == 0) else seq_len
    y = _peer(h2t, wts["u_tab"], wts["vt_tab"], at, cnt, r2, bt, x1, mod3, wts["g_post_ffn"],
              seq_len, tm, PEER_TE)

    assert valid >= CONV_W - 1
    conv_new = xbc.reshape(nb, seq_len, CONV_DIM)[:, valid - (CONV_W - 1):valid]
    return (y.reshape(nb, seq_len, d)[:, :valid],
            k.reshape(nb, seq_len, H_ATT, HEAD_DIM)[:, :valid],
            v.reshape(nb, seq_len, H_ATT, HEAD_DIM)[:, :valid],
            ssm_new, conv_new)


def _pad_lanes(a, width):
    return jnp.pad(a, ((0, 0), (0, width - a.shape[1])))


def kernel(x_prompt, x_sample, c_prompt, c_sample, cache_k, cache_v, state_ssm, state_conv, w_ada, b_ada, g_pre_mix, g_post_mix, g_pre_ffn, g_post_ffn, w_in, conv_w, conv_b, dt_bias, a_log, d_skip, g_attn_norm, g_ssd_norm, w_out, w_pq, sub_keys, u_tab, v_tab):
    depth = w_ada.shape[0]
    bp, _, d = x_prompt.shape
    bs, dec_len, _ = x_sample.shape
    d_in = w_in.shape[2]
    d_in_p = -(-d_in // LANES) * LANES
    pad_len = -(-dec_len // ROW_TILE) * ROW_TILE

    xp = x_prompt
    xq = jnp.pad(x_sample, ((0, 0), (0, pad_len - dec_len), (0, 0)))
    outs = [[] for _ in range(8)]
    for i in range(depth):
        wts = dict(
            g_pre_mix=g_pre_mix[i][None], g_post_mix=g_post_mix[i][None],
            g_pre_ffn=g_pre_ffn[i][None], g_post_ffn=g_post_ffn[i][None],
            w_in=_pad_lanes(w_in[i], d_in_p).astype(BF16),
            conv_w=conv_w[i].reshape(CONV_W, CONV_DIM), conv_b=conv_b[i][None],
            dt_bias=_pad_lanes(dt_bias[i][None], LANES), a_log=_pad_lanes(a_log[i][None], LANES),
            d_skip=jnp.repeat(d_skip[i], SSD_P)[None],
            g_attn_norm=g_attn_norm[i][None], g_ssd_norm=g_ssd_norm[i][None],
            w_out=w_out[i].astype(BF16), w_pq=w_pq[i].astype(BF16),
            sub_keys=sub_keys[i].astype(BF16),
            u_tab=u_tab[i].astype(BF16), vt_tab=v_tab[i].T.astype(BF16),
        )
        mod = _ada(jnp.concatenate([c_prompt, c_sample], axis=0), w_ada[i], b_ada[i])
        ssm_zero = jnp.zeros((bp, H_SSD, SSD_P, SSD_N), F32)
        conv_zero = jnp.zeros((bp, CONV_W - 1, CONV_DIM), F32)
        xp, k1, v1, s1, c1 = _layer(xp, mod[:bp], None, None, ssm_zero, conv_zero,
                                    xp.shape[1], wts)
        xq_full, k2, v2, s2, c2 = _layer(xq, mod[bp:], cache_k[i], cache_v[i], state_ssm[i],
                                         state_conv[i], dec_len, wts)
        xq = jnp.pad(xq_full, ((0, 0), (0, pad_len - dec_len), (0, 0))) if i + 1 < depth else xq_full
        for lst, val in zip(outs, (k1, v1, s1, c1, k2, v2, s2, c2)):
            lst.append(val)
    return (xp, xq) + tuple(jnp.stack(o) for o in outs)
```

```python
import functools

import jax
import jax.numpy as jnp
from jax import lax
from jax.experimental import pallas as pl
from jax.experimental.pallas import tpu as pltpu

F32 = jnp.float32
BF16 = jnp.bfloat16
EPS = 1e-6

H_ATT = 8
HEAD_DIM = 64
D_ATT = H_ATT * HEAD_DIM
H_SSD = 8
SSD_P = 64
D_SSD = H_SSD * SSD_P
SSD_GROUPS = 2
HEADS_PER_GROUP = H_SSD // SSD_GROUPS
SSD_N = 64
CONV_W = 4
CONV_DIM = D_SSD + 2 * SSD_GROUPS * SSD_N
N_KEYS = 128
PEER_HEADS = 8
PEER_TOPK = 16
PEER_HALF = 64

LANES = 128
SUBLANES = 8
PACKED_ROWS = 2 * SUBLANES
ROW_TILE = 128
VMEM_LIMIT = 56 * 1024 * 1024

NT_DIMS = (((1,), (1,)), ((), ()))
LOG2E = 1.4426950408889634
MASKED_SCORE = -1e30
ATTN_TQ = 1024
ATTN_TK = 256
ATTN_UNROLL = 4
SSD_CHUNK = 256


def _cparams(sem):
    return pltpu.CompilerParams(dimension_semantics=sem, vmem_limit_bytes=VMEM_LIMIT)


def _split3(a):
    hi = a.astype(BF16)
    r = a - hi.astype(F32)
    mid = r.astype(BF16)
    lo = (r - mid.astype(F32)).astype(BF16)
    return hi, mid, lo


def _dot_f32(a, b):
    a0, a1, a2 = _split3(a)
    b0, b1, b2 = _split3(b)
    d = functools.partial(jnp.dot, preferred_element_type=F32)
    return (d(a0, b0) + (d(a0, b1) + d(a1, b0))
            + (d(a0, b2) + d(a1, b1) + d(a2, b0)))


def _dot_01(m01, x):
    x0, x1, x2 = _split3(x)
    d = functools.partial(jnp.dot, preferred_element_type=F32)
    return d(m01, x0) + d(m01, x1) + d(m01, x2)


def _sigmoid(x):
    return 1.0 / (1.0 + jnp.exp(-x))


def _softplus(x):
    return jnp.maximum(x, 0.0) + jnp.log(1.0 + jnp.exp(-jnp.abs(x)))


def _rms(x):
    return x * lax.rsqrt(jnp.mean(x * x, axis=-1, keepdims=True) + EPS)


def _ada_kernel(c_ref, w_ref, b_ref, o_ref):
    c = c_ref[...]
    o_ref[...] = _dot_f32(c * _sigmoid(c), w_ref[...]) + b_ref[...]


def _ada(c, w_ada, b_ada):
    nb, d = c.shape
    n_out = w_ada.shape[1]
    bn = d
    return pl.pallas_call(
        _ada_kernel,
        grid=(n_out // bn,),
        in_specs=[pl.BlockSpec((nb, d), lambda j: (0, 0)),
                  pl.BlockSpec((d, bn), lambda j: (0, j)),
                  pl.BlockSpec((1, bn), lambda j: (0, j))],
        out_specs=pl.BlockSpec((nb, bn), lambda j: (0, j)),
        out_shape=jax.ShapeDtypeStruct((nb, n_out), F32),
        compiler_params=_cparams(("arbitrary",)),
        name="ada",
    )(c, w_ada, b_ada.reshape(1, n_out))


def _inproj_kernel(x_ref, mod_ref, g_ref, w_ref, q_ref, k_ref, v_ref, z_ref, xbc_ref, dt_ref, *, d):
    mod = mod_ref[0]
    sh1 = mod[:, 0:d]
    sc1 = mod[:, d:2 * d]
    h = (_rms(x_ref[...]) * g_ref[...] * (1.0 + sc1) + sh1).astype(BF16)

    def proj(lo, hi):
        return jnp.dot(h, w_ref[:, lo:hi], preferred_element_type=F32)

    o = 0
    q_ref[...] = (proj(o, o + D_ATT) * (HEAD_DIM ** -0.5 * LOG2E)).astype(BF16)
    o += D_ATT
    k_ref[...] = proj(o, o + D_ATT)
    o += D_ATT
    v_ref[...] = proj(o, o + D_ATT)
    o += D_ATT
    z_ref[...] = proj(o, o + D_SSD)
    o += D_SSD
    xbc_ref[...] = proj(o, o + CONV_DIM)
    o += CONV_DIM
    dt_ref[...] = proj(o, o + LANES)


def _inproj(xf, mod3, g_pre, w_in_p, seq_len, tm):
    n, d = xf.shape
    tiles_per_seq = seq_len // tm
    row = lambda i: (i, 0)
    full = lambda i: (0, 0)
    widths = (D_ATT, D_ATT, D_ATT, D_SSD, CONV_DIM, LANES)
    dtypes = (BF16, F32, F32, F32, F32, F32)
    return pl.pallas_call(
        functools.partial(_inproj_kernel, d=d),
        grid=(n // tm,),
        in_specs=[pl.BlockSpec((tm, d), row),
                  pl.BlockSpec((1, 1, 6 * d), lambda i: (i // tiles_per_seq, 0, 0)),
                  pl.BlockSpec((1, d), full),
                  pl.BlockSpec(w_in_p.shape, full)],
        out_specs=[pl.BlockSpec((tm, w), row) for w in widths],
        out_shape=[jax.ShapeDtypeStruct((n, w), t) for w, t in zip(widths, dtypes)],
        compiler_params=_cparams(("arbitrary",)),
        name="inproj",
    )(xf, mod3, g_pre, w_in_p)


def _attn_kernel(q_ref, kd_ref, vd_ref, kp_ref, vp_ref, o_ref, kbf, vbf, oacc, carry, *,
                 tq, tk, n_prev_static):
    qi = pl.program_id(2)
    lp = kp_ref.shape[0]
    cast_rows = min(lp, 512)

    prev_reused = n_prev_static is None

    if prev_reused:
        @pl.when(qi == 0)
        def _():
            def cp(c, _):
                rows = pl.ds(pl.multiple_of(c * cast_rows, cast_rows), cast_rows)
                for h in range(2):
                    lanes = slice(h * HEAD_DIM, (h + 1) * HEAD_DIM)
                    kbf[h, rows, :] = kp_ref[rows, lanes].astype(BF16)
                    vbf[h, rows, :] = vp_ref[rows, lanes].astype(BF16)
                return 0
            lax.fori_loop(0, lp // cast_rows, cp, 0)

    td = min(tq, tk)
    kk = lax.broadcasted_iota(jnp.int32, (tk, tk), 0)
    nn = lax.broadcasted_iota(jnp.int32, (tk, tk), 1)
    msum = jnp.where(kk > nn, 1.0, 0.0).astype(BF16)

    oacc[...] = jnp.zeros_like(oacc)
    carry[...] = jnp.zeros_like(carry)

    def step(h, rows, z2, v):
        nk = z2.shape[1]
        zb = z2.astype(BF16)
        l2 = jnp.log(1.0 + jnp.exp2(-jnp.abs(zb))) * LOG2E
        soft = jnp.maximum(zb, 0.0) + l2
        log_beta = jnp.minimum(zb, 0.0) - l2
        newer = jnp.dot(soft, msum[:nk, :nk], preferred_element_type=F32)
        c = carry[h, rows, :]
        gone = (newer + jnp.concatenate([c] * (nk // LANES), axis=1)).astype(BF16)
        w = jnp.exp2(log_beta - gone)
        carry[h, rows, :] = c + (newer[:, 0:1] + soft[:, 0:1].astype(F32))
        oacc[h, rows, :] += jnp.dot(w, v, preferred_element_type=F32)

    for jb in reversed(range(tq // td)):
        r0 = jb * td
        rows = slice(r0, tq)
        keys = slice(r0, r0 + td)
        rl = lax.broadcasted_iota(jnp.int32, (tq - r0, td), 0)
        cl = lax.broadcasted_iota(jnp.int32, (tq - r0, td), 1)
        causal = cl < rl
        for h in range(2):
            lanes = slice(h * HEAD_DIM, (h + 1) * HEAD_DIM)
            z2 = lax.dot_general(q_ref[rows, lanes], kd_ref[keys, lanes].astype(BF16), NT_DIMS,
                                 preferred_element_type=F32)
            step(h, rows, jnp.where(causal, z2, MASKED_SCORE), vd_ref[keys, lanes].astype(BF16))

    n_prev = qi * (tq // tk) if n_prev_static is None else n_prev_static
    prev_multiple = (tq // tk) if n_prev_static is None else n_prev_static
    unroll = next(u for u in (ATTN_UNROLL, 2, 1) if prev_multiple % u == 0)

    def body(jj, _):
        for u in range(unroll):
            keys = pl.ds(pl.multiple_of((n_prev - 1 - unroll * jj - u) * tk, tk), tk)
            for h in range(2):
                lanes = slice(h * HEAD_DIM, (h + 1) * HEAD_DIM)
                if prev_reused:
                    kb, vb = kbf[h, keys, :], vbf[h, keys, :]
                else:
                    kb = kp_ref[keys, lanes].astype(BF16)
                    vb = vp_ref[keys, lanes].astype(BF16)
                z2 = lax.dot_general(q_ref[:, lanes], kb, NT_DIMS, preferred_element_type=F32)
                step(h, slice(0, tq), z2, vb)
        return 0

    lax.fori_loop(0, n_prev // unroll, body, 0)
    o_ref[...] = jnp.concatenate([oacc[0], oacc[1]], axis=1)


def _attention(q, k, v, k_prev, v_prev, nb, seq_len, prev_len, prev_is_self, tq, tk):
    n = q.shape[0]
    nq = seq_len // tq
    assert seq_len % tq == 0 and prev_len % tk == 0 and (tq % tk == 0 or tk % tq == 0)
    blk = lambda b, hp, qi: (b * nq + qi, hp)
    prev = lambda b, hp, qi: (b, hp)
    kern = functools.partial(_attn_kernel, tq=tq, tk=tk,
                             n_prev_static=None if prev_is_self else prev_len // tk)
    return pl.pallas_call(
        kern,
        grid=(nb, H_ATT // 2, nq),
        in_specs=[pl.BlockSpec((tq, LANES), blk),
                  pl.BlockSpec((tq, LANES), blk),
                  pl.BlockSpec((tq, LANES), blk),
                  pl.BlockSpec((prev_len, LANES), prev),
                  pl.BlockSpec((prev_len, LANES), prev)],
        out_specs=pl.BlockSpec((tq, LANES), blk),
        out_shape=jax.ShapeDtypeStruct((n, D_ATT), F32),
        scratch_shapes=[pltpu.VMEM((2, prev_len, HEAD_DIM), BF16),
                        pltpu.VMEM((2, prev_len, HEAD_DIM), BF16),
                        pltpu.VMEM((2, tq, HEAD_DIM), F32),
                        pltpu.VMEM((2, tq, LANES), F32)],
        compiler_params=_cparams(("arbitrary", "arbitrary", "arbitrary")),
        name="attn",
    )(q, k, v, k_prev, v_prev)


def _ssd_kernel(xbc_ref, dt_ref, z_ref, conv0_ref, ssm0_ref, cw_ref, cb_ref, dtb_ref, alog_ref,
                dsk_ref, g_ref, y_ref, ssm_ref, buf, state, *, q, valid):
    ci = pl.program_id(1)
    hist = SUBLANES

    @pl.when(ci == 0)
    def _():
        buf[0:hist, :] = conv0_ref[0]
        state[...] = ssm0_ref[0]

    buf[hist:hist + q, :] = xbc_ref[...]
    cw = cw_ref[...]
    acc = cb_ref[...] + cw[CONV_W - 1:CONV_W, :] * buf[hist:hist + q, :]
    for s in range(1, CONV_W):
        acc = acc + cw[CONV_W - 1 - s:CONV_W - s, :] * buf[hist - s:hist - s + q, :]
    xc = acc * _sigmoid(acc)
    buf[0:hist, :] = buf[q:q + hist, :]

    lane = lax.broadcasted_iota(jnp.int32, (q, LANES), 1)
    rowi = lax.broadcasted_iota(jnp.int32, (q, LANES), 0)
    dt = _softplus(dt_ref[...] + dtb_ref[...])
    dt = jnp.where(lane < H_SSD, dt, 0.0)
    if valid < q:
        dt = jnp.where(rowi < valid, dt, 0.0)
    d_a = dt * (-jnp.exp(alog_ref[...]))

    tt = lax.broadcasted_iota(jnp.int32, (q, q), 0)
    ss = lax.broadcasted_iota(jnp.int32, (q, q), 1)
    causal = ss <= tt
    cum = _dot_01(jnp.where(causal, 1.0, 0.0).astype(BF16), d_a)
    cum_t = cum.T
    last = cum[valid - 1:valid, :]
    e_cum = jnp.exp(cum)
    e_end = jnp.exp(last - cum)
    e_last = jnp.exp(last)

    ys = []
    for g in range(SSD_GROUPS):
        b_off = D_SSD + g * SSD_N
        c_off = D_SSD + SSD_GROUPS * SSD_N + g * SSD_N
        bg = xc[:, b_off:b_off + SSD_N].astype(BF16)
        cg = xc[:, c_off:c_off + SSD_N].astype(BF16)
        cb = lax.dot_general(cg, bg, NT_DIMS, preferred_element_type=F32)
        for pair in range(HEADS_PER_GROUP // 2):
            xws = []
            for hh in range(2):
                h = g * HEADS_PER_GROUP + pair * 2 + hh
                xh = xc[:, h * SSD_P:(h + 1) * SSD_P]
                seg = cum[:, h:h + 1] - cum_t[h:h + 1, :]
                decay = jnp.exp(jnp.where(causal, seg, -jnp.inf))
                xdt = xh * dt[:, h:h + 1]
                y_diag = jnp.dot((cb * decay).astype(BF16), xdt.astype(BF16),
                                 preferred_element_type=F32)
                st = state[h]
                y_off = lax.dot_general(cg, st.astype(BF16), NT_DIMS,
                                        preferred_element_type=F32) * e_cum[:, h:h + 1]
                ys.append(y_diag + y_off + dsk_ref[:, h * SSD_P:(h + 1) * SSD_P] * xh)
                xws.append(xdt * e_end[:, h:h + 1])
            xw_t = jnp.concatenate(xws, axis=1).T.astype(BF16)
            for hh in range(2):
                h = g * HEADS_PER_GROUP + pair * 2 + hh
                upd = jnp.dot(xw_t[hh * SSD_P:(hh + 1) * SSD_P, :], bg, preferred_element_type=F32)
                state[h] = state[h] * e_last[:, h:h + 1] + upd
    y = jnp.concatenate(ys, axis=1)
    zz = z_ref[...]
    y_ref[...] = _rms(y * (zz * _sigmoid(zz))) * g_ref[...]

    @pl.when(ci == pl.num_programs(1) - 1)
    def _():
        ssm_ref[0] = state[...]


def _ssd(xbc, dt, z, conv0_p, ssm0, cw, cb, dtb, alog, dsk, g_ssd, nb, seq_len, q, valid):
    n = xbc.shape[0]
    nc = seq_len // q
    row = lambda b, c: (b * nc + c, 0)
    full2 = lambda b, c: (0, 0)
    return pl.pallas_call(
        functools.partial(_ssd_kernel, q=q, valid=valid),
        grid=(nb, nc),
        in_specs=[pl.BlockSpec((q, CONV_DIM), row),
                  pl.BlockSpec((q, LANES), row),
                  pl.BlockSpec((q, D_SSD), row),
                  pl.BlockSpec((1, SUBLANES, CONV_DIM), lambda b, c: (b, 0, 0)),
                  pl.BlockSpec((1, H_SSD, SSD_P, SSD_N), lambda b, c: (b, 0, 0, 0)),
                  pl.BlockSpec((CONV_W, CONV_DIM), full2),
                  pl.BlockSpec((1, CONV_DIM), full2),
                  pl.BlockSpec((1, LANES), full2),
                  pl.BlockSpec((1, LANES), full2),
                  pl.BlockSpec((1, D_SSD), full2),
                  pl.BlockSpec((1, D_SSD), full2)],
        out_specs=[pl.BlockSpec((q, D_SSD), row),
                   pl.BlockSpec((1, H_SSD, SSD_P, SSD_N), lambda b, c: (b, 0, 0, 0))],
        out_shape=[jax.ShapeDtypeStruct((n, D_SSD), F32),
                   jax.ShapeDtypeStruct((nb, H_SSD, SSD_P, SSD_N), F32)],
        scratch_shapes=[pltpu.VMEM((q + SUBLANES, CONV_DIM), F32),
                        pltpu.VMEM((H_SSD, SSD_P, SSD_N), F32)],
        compiler_params=_cparams(("arbitrary", "arbitrary")),
        name="ssd",
    )(xbc, dt, z, conv0_p, ssm0, cw, cb, dtb, alog, dsk, g_ssd)


def _top_values(s, count):
    outs = []
    cur = s
    for r in range(count):
        m = jnp.max(cur, axis=0, keepdims=True)
        outs.append(m)
        if r + 1 < count:
            cur = jnp.where(cur == m, -jnp.inf, cur)
    return outs


def _twin_bf16(x):
    bits = pltpu.bitcast(x.astype(BF16).astype(F32), jnp.uint32)
    return bits | (bits >> 16)


INT32_MIN = -2 ** 31
GATE_STEP = 2.0 ** 64


def _order_key(x):
    bits = pltpu.bitcast(x, jnp.int32)
    return bits ^ ((bits >> 31) & 0x7FFFFFFF)


def _order_key_inverse(key):
    return pltpu.bitcast(key ^ ((key >> 31) & 0x7FFFFFFF), F32)


def _top_ranked(s, count):
    outs = []
    cur = _order_key(s)
    for r in range(count):
        m = jnp.max(cur, axis=0, keepdims=True)
        outs.append(_order_key_inverse(m))
        cur = jnp.where(cur == m, INT32_MIN + r, cur)
    rank = jnp.where(cur < INT32_MIN + count, cur - INT32_MIN, count).astype(F32)
    return outs, rank


def _mid_kernel(x_ref, oa_ref, os_ref, mod_ref, ga_ref, gpm_ref, gpf_ref, wo_ref, wpq_ref, sk_ref,
                x1_ref, h2t_ref, at_ref, cnt_ref, r2_ref, bt_ref, *, d):
    mod = mod_ref[0]
    gt1 = mod[:, 2 * d:3 * d]
    sh2 = mod[:, 3 * d:4 * d]
    sc2 = mod[:, 4 * d:5 * d]
    attn_o = (_rms(oa_ref[...]) * ga_ref[...]).astype(BF16)
    mix = (jnp.dot(attn_o, wo_ref[0:D_ATT, :], preferred_element_type=F32)
           + jnp.dot(os_ref[...].astype(BF16), wo_ref[D_ATT:D_ATT + D_SSD, :],
                     preferred_element_type=F32))
    x1 = x_ref[...] + gt1 * (_rms(mix) * gpm_ref[...])
    x1_ref[...] = x1
    h2 = _rms(x1) * gpf_ref[...] * (1.0 + sc2) + sh2
    h2t_ref[...] = h2.T.astype(BF16)
    qh = jnp.dot(h2.astype(BF16), wpq_ref[...], preferred_element_type=F32).astype(BF16)
    k1 = sk_ref[0]
    k2 = sk_ref[1]
    for h in range(PEER_HEADS):
        base = h * 2 * PEER_HALF
        s1 = lax.dot_general(k1, qh[:, base:base + PEER_HALF], NT_DIMS,
                             preferred_element_type=F32)
        s2 = lax.dot_general(k2, qh[:, base + PEER_HALF:base + 2 * PEER_HALF], NT_DIMS,
                             preferred_element_type=F32)
        t1, rank1 = _top_ranked(s1, PEER_TOPK)
        t2, rank2 = _top_ranked(s2, PEER_TOPK)
        half = PEER_TOPK // 2
        t1_lo = jnp.concatenate(t1[:half], axis=0)
        t1_hi = jnp.concatenate(t1[half:], axis=0)
        t2_hi = jnp.concatenate(t2[half:], axis=0)
        rank_a = lax.broadcasted_iota(jnp.int32, t1_lo.shape, 0)
        blocks = [t1_lo + t2[0], t1_hi + t2[0], t1[0] + t2_hi]
        for b in range(1, half):
            blocks.append(jnp.where(rank_a < PEER_TOPK // (b + 1), t1_lo + t2[b], -jnp.inf))
        best = _top_values(jnp.concatenate(blocks, axis=0), PEER_TOPK)
        tau = best[PEER_TOPK - 1]
        zsum = jnp.ones_like(best[0])
        for r in range(1, PEER_TOPK):
            zsum = zsum + jnp.exp(best[r] - best[0])
        taken = [jnp.where(blk >= tau, 1.0, 0.0) for blk in blocks]
        cnt_lo = taken[0]
        for blk in taken[3:]:
            cnt_lo = cnt_lo + blk
        a0_extra = jnp.sum(taken[2], axis=0, keepdims=True)
        cnt_lo = cnt_lo + jnp.where(rank_a == 0, a0_extra, 0.0)
        cnt_by_rank = jnp.concatenate([cnt_lo, taken[1]], axis=0)
        cnt = jnp.zeros_like(s1)
        for a in range(PEER_TOPK):
            cnt = jnp.where(rank1 == float(a), cnt_by_rank[a:a + 1, :], cnt)
        at_ref[h] = _twin_bf16(jnp.exp(s1 - t1[0]) * (0.5 / zsum))
        cnt_ref[h] = _twin_bf16(cnt * GATE_STEP)
        r2_ref[h] = rank2 * GATE_STEP
        bt_ref[h] = jnp.exp(s2 - t2[0])


def _mid(xf, o_att, o_ssd, mod3, g_attn, g_post_mix, g_pre_ffn, w_out, w_pq, sub_keys, seq_len, tm):
    n, d = xf.shape
    tiles_per_seq = seq_len // tm
    row = lambda i: (i, 0)
    full = lambda i: (0, 0)
    colblk = lambda i: (0, 0, i)
    gate_u32 = jax.ShapeDtypeStruct((PEER_HEADS, N_KEYS, n), jnp.uint32)
    gate_f32 = jax.ShapeDtypeStruct((PEER_HEADS, N_KEYS, n), F32)
    gate_spec = pl.BlockSpec((PEER_HEADS, N_KEYS, tm), colblk)
    return pl.pallas_call(
        functools.partial(_mid_kernel, d=d),
        grid=(n // tm,),
        in_specs=[pl.BlockSpec((tm, d), row),
                  pl.BlockSpec((tm, D_ATT), row),
                  pl.BlockSpec((tm, D_SSD), row),
                  pl.BlockSpec((1, 1, 6 * d), lambda i: (i // tiles_per_seq, 0, 0)),
                  pl.BlockSpec((1, D_ATT), full),
                  pl.BlockSpec((1, d), full),
                  pl.BlockSpec((1, d), full),
                  pl.BlockSpec(w_out.shape, full),
                  pl.BlockSpec(w_pq.shape, full),
                  pl.BlockSpec(sub_keys.shape, lambda i: (0, 0, 0))],
        out_specs=[pl.BlockSpec((tm, d), row),
                   pl.BlockSpec((d, tm), lambda i: (0, i)),
                   gate_spec, gate_spec, gate_spec, gate_spec],
        out_shape=[jax.ShapeDtypeStruct((n, d), F32),
                   jax.ShapeDtypeStruct((d, n), BF16),
                   gate_u32, gate_u32, gate_f32, gate_f32],
        compiler_params=_cparams(("arbitrary",)),
        name="mid",
    )(xf, o_att, o_ssd, mod3, g_attn, g_post_mix, g_pre_ffn, w_out, w_pq, sub_keys)


GELU_C = 0.7978845608028654
PEER_SUB = 256
PEER_SUBS_PER_ACC = 2
PEER_TE = 2048


def _twin_rows(word_row, rows):
    tile = pltpu.bitcast(jnp.broadcast_to(word_row, (SUBLANES, word_row.shape[1])), BF16)
    return jnp.concatenate([tile] * (rows // PACKED_ROWS), axis=0)


def _gelu_tanh_x2(x):
    return x * (1.0 + jnp.tanh(x * ((x * x) * (GELU_C * 0.044715) + GELU_C)))


def _peer_kernel(h2t_ref, u_ref, vt_ref, at_ref, cnt_ref, r2_ref, bt_ref, x1_ref, mod_ref, g_ref,
                 o_ref, acc, r2s, bts, *, d, tm, te, seq_rows):
    j = pl.program_id(1)

    @pl.when(j == 0)
    def _():
        acc[...] = jnp.zeros_like(acc)
        for h in range(PEER_HEADS):
            r2s[h] = r2_ref[h].astype(BF16)
            bts[h] = bt_ref[h].astype(BF16)

    i_per_sub = PEER_SUB // N_KEYS
    n_sub = te // PEER_SUB

    def scores(s):
        return jnp.dot(u_ref[s * PEER_SUB:(s + 1) * PEER_SUB, :], h2t_ref[...],
                       preferred_element_type=F32)

    sc = scores(0)
    pending = []
    for s in range(n_sub):
        sc_next = scores(s + 1) if s + 1 < n_sub else None
        act = _gelu_tanh_x2(sc.astype(BF16))
        lane_blocks = []
        for lg in range(tm // LANES):
            lanes = slice(lg * LANES, (lg + 1) * LANES)
            gsums = [None] * i_per_sub
            for h in range(PEER_HEADS):
                r2 = r2s[h, :, lanes]
                b = bts[h, :, lanes]
                for il in range(i_per_sub):
                    ii = s * i_per_sub + il
                    cnt = _twin_rows(cnt_ref[h, ii:ii + 1, lanes], N_KEYS)
                    a = _twin_rows(at_ref[h, ii:ii + 1, lanes], N_KEYS)
                    term = jnp.minimum(b * a, jnp.maximum(cnt - r2, 0.0))
                    gsums[il] = term if gsums[il] is None else gsums[il] + term
            lane_blocks.append(jnp.concatenate(
                [act[il * N_KEYS:(il + 1) * N_KEYS, lanes] * gsums[il] for il in range(i_per_sub)],
                axis=0))
        pending.append(jnp.concatenate(lane_blocks, axis=1))
        if len(pending) == PEER_SUBS_PER_ACC or s + 1 == n_sub:
            first = s + 1 - len(pending)
            acc[...] += jnp.dot(vt_ref[:, first * PEER_SUB:(s + 1) * PEER_SUB],
                                jnp.concatenate(pending, axis=0), preferred_element_type=F32)
            pending = []
        sc = sc_next

    @pl.when(j == pl.num_programs(1) - 1)
    def _():
        f = _rms(acc[...].T) * g_ref[...]
        for s in range(tm // seq_rows):
            rows = slice(s * seq_rows, (s + 1) * seq_rows)
            gt2 = mod_ref[s][:, 5 * d:6 * d]
            o_ref[rows, :] = x1_ref[rows, :] + gt2 * f[rows, :]


def _peer(h2t, u_bf, vt_bf, at, cnt, r2, bt, x1, mod3, g_post_ffn, seq_len, tm, te):
    d, n = h2t.shape
    n_exp = u_bf.shape[0]
    seq_rows = min(seq_len, tm)
    seqs_per_tile = tm // seq_rows
    gate_spec = pl.BlockSpec((PEER_HEADS, N_KEYS, tm), lambda i, j: (0, 0, i))
    irow_spec = pl.BlockSpec((PEER_HEADS, te // N_KEYS, tm), lambda i, j: (0, j, i))
    assert (te // N_KEYS) % SUBLANES == 0
    if seqs_per_tile == 1:
        tiles_per_seq = seq_len // tm
        mod_spec = pl.BlockSpec((1, 1, 6 * d), lambda i, j: (i // tiles_per_seq, 0, 0))
    else:
        mod_spec = pl.BlockSpec((seqs_per_tile, 1, 6 * d), lambda i, j: (i, 0, 0))
    return pl.pallas_call(
        functools.partial(_peer_kernel, d=d, tm=tm, te=te, seq_rows=seq_rows),
        grid=(n // tm, n_exp // te),
        in_specs=[pl.BlockSpec((d, tm), lambda i, j: (0, i)),
                  pl.BlockSpec((te, d), lambda i, j: (j, 0)),
                  pl.BlockSpec((d, te), lambda i, j: (0, j)),
                  irow_spec, irow_spec, gate_spec, gate_spec,
                  pl.BlockSpec((tm, d), lambda i, j: (i, 0)),
                  mod_spec,
                  pl.BlockSpec((1, d), lambda i, j: (0, 0))],
        out_specs=pl.BlockSpec((tm, d), lambda i, j: (i, 0)),
        out_shape=jax.ShapeDtypeStruct((n, d), F32),
        scratch_shapes=[pltpu.VMEM((d, tm), F32),
                        pltpu.VMEM((PEER_HEADS, N_KEYS, tm), BF16),
                        pltpu.VMEM((PEER_HEADS, N_KEYS, tm), BF16)],
        compiler_params=_cparams(("arbitrary", "arbitrary")),
        name="peer",
    )(h2t, u_bf, vt_bf, at, cnt, r2, bt, x1, mod3, g_post_ffn)


def _pick_tile(seq_len, target):
    t = min(seq_len, target)
    assert seq_len % t == 0 and t % ROW_TILE == 0
    return t


def _layer(x, mod, k_past, v_past, ssm0, conv0, valid, wts):
    nb, seq_len, d = x.shape
    n = nb * seq_len
    xf = x.reshape(n, d)
    mod3 = mod.reshape(nb, 1, 6 * d)

    q, k, v, z, xbc, dt = _inproj(xf, mod3, wts["g_pre_mix"], wts["w_in"], seq_len,
                                  _pick_tile(seq_len, 512))
    tq = _pick_tile(seq_len, ATTN_TQ)
    if k_past is None:
        o_att = _attention(q, k, v, k, v, nb, seq_len, seq_len, True, tq, min(tq, ATTN_TK))
    else:
        past = k_past.shape[1]
        o_att = _attention(q, k, v, k_past.reshape(nb * past, D_ATT), v_past.reshape(nb * past, D_ATT),
                           nb, seq_len, past, False, tq, ATTN_TK)

    conv0_p = jnp.pad(conv0, ((0, 0), (SUBLANES - (CONV_W - 1), 0), (0, 0)))
    chunk = _pick_tile(seq_len, SSD_CHUNK)
    assert valid == seq_len or seq_len == chunk
    o_ssd, ssm_new = _ssd(xbc, dt, z, conv0_p, ssm0, wts["conv_w"], wts["conv_b"], wts["dt_bias"],
                          wts["a_log"], wts["d_skip"], wts["g_ssd_norm"], nb, seq_len, chunk,
                          min(valid, chunk))

    x1, h2t, at, cnt, r2, bt = _mid(xf, o_att, o_ssd, mod3, wts["g_attn_norm"], wts["g_post_mix"],
                                    wts["g_pre_ffn"], wts["w_out"], wts["w_pq"], wts["sub_keys"],
                                    seq_len, _pick_tile(seq_len, 256))
    tm = 512 if n % 512 == 0 and (seq_len % 512 == 0 or 512 % seq_len == 0) else seq_len
    y = _peer(h2t, wts["u_tab"], wts["vt_tab"], at, cnt, r2, bt, x1, mod3, wts["g_post_ffn"],
              seq_len, tm, PEER_TE)

    assert valid >= CONV_W - 1
    conv_new = xbc.reshape(nb, seq_len, CONV_DIM)[:, valid - (CONV_W - 1):valid]
    return (y.reshape(nb, seq_len, d)[:, :valid],
            k.reshape(nb, seq_len, H_ATT, HEAD_DIM)[:, :valid],
            v.reshape(nb, seq_len, H_ATT, HEAD_DIM)[:, :valid],
            ssm_new, conv_new)


def _pad_lanes(a, width):
    return jnp.pad(a, ((0, 0), (0, width - a.shape[1])))


def kernel(x_prompt, x_sample, c_prompt, c_sample, cache_k, cache_v, state_ssm, state_conv, w_ada, b_ada, g_pre_mix, g_post_mix, g_pre_ffn, g_post_ffn, w_in, conv_w, conv_b, dt_bias, a_log, d_skip, g_attn_norm, g_ssd_norm, w_out, w_pq, sub_keys, u_tab, v_tab):
    depth = w_ada.shape[0]
    bp, _, d = x_prompt.shape
    bs, dec_len, _ = x_sample.shape
    d_in = w_in.shape[2]
    d_in_p = -(-d_in // LANES) * LANES
    pad_len = -(-dec_len // ROW_TILE) * ROW_TILE

    xp = x_prompt
    xq = jnp.pad(x_sample, ((0, 0), (0, pad_len - dec_len), (0, 0)))
    outs = [[] for _ in range(8)]
    for i in range(depth):
        wts = dict(
            g_pre_mix=g_pre_mix[i][None], g_post_mix=g_post_mix[i][None],
            g_pre_ffn=g_pre_ffn[i][None], g_post_ffn=g_post_ffn[i][None],
            w_in=_pad_lanes(w_in[i], d_in_p).astype(BF16),
            conv_w=conv_w[i].reshape(CONV_W, CONV_DIM), conv_b=conv_b[i][None],
            dt_bias=_pad_lanes(dt_bias[i][None], LANES), a_log=_pad_lanes(a_log[i][None], LANES),
            d_skip=jnp.repeat(d_skip[i], SSD_P)[None],
            g_attn_norm=g_attn_norm[i][None], g_ssd_norm=g_ssd_norm[i][None],
            w_out=w_out[i].astype(BF16), w_pq=w_pq[i].astype(BF16),
            sub_keys=sub_keys[i].astype(BF16),
            u_tab=u_tab[i].astype(BF16), vt_tab=v_tab[i].T.astype(BF16),
        )
        mod = _ada(jnp.concatenate([c_prompt, c_sample], axis=0), w_ada[i], b_ada[i])
        ssm_zero = jnp.zeros((bp, H_SSD, SSD_P, SSD_N), F32)
        conv_zero = jnp.zeros((bp, CONV_W - 1, CONV_DIM), F32)
        xp, k1, v1, s1, c1 = _layer(xp, mod[:bp], None, None, ssm_zero, conv_zero,
                                    xp.shape[1], wts)
        xq_full, k2, v2, s2, c2 = _layer(xq, mod[bp:], cache_k[i], cache_v[i], state_ssm[i],
                                         state_conv[i], dec_len, wts)
        xq = jnp.pad(xq_full, ((0, 0), (0, pad_len - dec_len), (0, 0))) if i + 1 < depth else xq_full
        for lst, val in zip(outs, (k1, v1, s1, c1, k2, v2, s2, c2)):
            lst.append(val)
    return (xp, xq) + tuple(jnp.stack(o) for o in outs)
```

```python
import functools

import jax
import jax.numpy as jnp
from jax import lax
from jax.experimental import pallas as pl
from jax.experimental.pallas import tpu as pltpu

F32 = jnp.float32
BF16 = jnp.bfloat16
EPS = 1e-6

H_ATT = 8
HEAD_DIM = 64
D_ATT = H_ATT * HEAD_DIM
H_SSD = 8
SSD_P = 64
D_SSD = H_SSD * SSD_P
SSD_GROUPS = 2
HEADS_PER_GROUP = H_SSD // SSD_GROUPS
SSD_N = 64
CONV_W = 4
CONV_DIM = D_SSD + 2 * SSD_GROUPS * SSD_N
N_KEYS = 128
PEER_HEADS = 8
PEER_TOPK = 16
PEER_HALF = 64

LANES = 128
SUBLANES = 8
PACKED_ROWS = 2 * SUBLANES
ROW_TILE = 128
VMEM_LIMIT = 56 * 1024 * 1024

NT_DIMS = (((1,), (1,)), ((), ()))
LOG2E = 1.4426950408889634
MASKED_SCORE = -1e30
ATTN_TQ = 1024
ATTN_TK = 256
ATTN_UNROLL = 4
SSD_CHUNK = 256
INPROJ_TM = 512
MID_TM = 256
PEER_TM = 512


def _cparams(sem):
    return pltpu.CompilerParams(dimension_semantics=sem, vmem_limit_bytes=VMEM_LIMIT)


def _split3(a):
    hi = a.astype(BF16)
    r = a - hi.astype(F32)
    mid = r.astype(BF16)
    lo = (r - mid.astype(F32)).astype(BF16)
    return hi, mid, lo


def _dot_f32(a, b):
    a0, a1, a2 = _split3(a)
    b0, b1, b2 = _split3(b)
    d = functools.partial(jnp.dot, preferred_element_type=F32)
    return (d(a0, b0) + (d(a0, b1) + d(a1, b0))
            + (d(a0, b2) + d(a1, b1) + d(a2, b0)))


def _dot_01(m01, x):
    x0, x1, x2 = _split3(x)
    d = functools.partial(jnp.dot, preferred_element_type=F32)
    return d(m01, x0) + d(m01, x1) + d(m01, x2)


def _sigmoid(x):
    return 1.0 / (1.0 + jnp.exp(-x))


def _softplus(x):
    return jnp.maximum(x, 0.0) + jnp.log(1.0 + jnp.exp(-jnp.abs(x)))


def _rms(x):
    return x * lax.rsqrt(jnp.mean(x * x, axis=-1, keepdims=True) + EPS)


def _ada_kernel(c_ref, w_ref, b_ref, o_ref):
    c = c_ref[...]
    o_ref[...] = _dot_f32(c * _sigmoid(c), w_ref[...]) + b_ref[...]


def _ada(c, w_ada, b_ada):
    nb, d = c.shape
    n_out = w_ada.shape[1]
    bn = d
    return pl.pallas_call(
        _ada_kernel,
        grid=(n_out // bn,),
        in_specs=[pl.BlockSpec((nb, d), lambda j: (0, 0)),
                  pl.BlockSpec((d, bn), lambda j: (0, j)),
                  pl.BlockSpec((1, bn), lambda j: (0, j))],
        out_specs=pl.BlockSpec((nb, bn), lambda j: (0, j)),
        out_shape=jax.ShapeDtypeStruct((nb, n_out), F32),
        compiler_params=_cparams(("arbitrary",)),
        name="ada",
    )(c, w_ada, b_ada.reshape(1, n_out))


def _inproj_kernel(x_ref, mod_ref, g_ref, w_ref, q_ref, k_ref, v_ref, z_ref, xbc_ref, dt_ref,
                   kh_ref, vh_ref, *, d):
    mod = mod_ref[0]
    sh1 = mod[:, 0:d]
    sc1 = mod[:, d:2 * d]
    h = (_rms(x_ref[...]) * g_ref[...] * (1.0 + sc1) + sh1).astype(BF16)

    def proj(lo, hi):
        return jnp.dot(h, w_ref[:, lo:hi], preferred_element_type=F32)

    o = 0
    q_ref[...] = (proj(o, o + D_ATT) * (HEAD_DIM ** -0.5 * LOG2E)).astype(BF16)
    o += D_ATT
    for dense_ref, heads_ref in ((k_ref, kh_ref), (v_ref, vh_ref)):
        p = proj(o, o + D_ATT)
        dense_ref[...] = p
        for hd in range(H_ATT):
            heads_ref[pl.ds(hd, p.shape[0], stride=H_ATT), :] = p[:, hd * HEAD_DIM:(hd + 1) * HEAD_DIM]
        o += D_ATT
    z_ref[...] = proj(o, o + D_SSD)
    o += D_SSD
    xbc_ref[...] = proj(o, o + CONV_DIM)
    o += CONV_DIM
    dt_ref[...] = proj(o, o + LANES)


def _inproj(xf, mod3, g_pre, w_in_p, seq_len, tm):
    n, d = xf.shape
    tiles_per_seq = seq_len // tm
    row = lambda i: (i, 0)
    full = lambda i: (0, 0)
    widths = (D_ATT, D_ATT, D_ATT, D_SSD, CONV_DIM, LANES)
    dtypes = (BF16, F32, F32, F32, F32, F32)
    heads_spec = pl.BlockSpec((tm * H_ATT, HEAD_DIM), row)
    heads_shape = jax.ShapeDtypeStruct((n * H_ATT, HEAD_DIM), F32)
    return pl.pallas_call(
        functools.partial(_inproj_kernel, d=d),
        grid=(n // tm,),
        in_specs=[pl.BlockSpec((tm, d), row),
                  pl.BlockSpec((1, 1, 6 * d), lambda i: (i // tiles_per_seq, 0, 0)),
                  pl.BlockSpec((1, d), full),
                  pl.BlockSpec(w_in_p.shape, full)],
        out_specs=[pl.BlockSpec((tm, w), row) for w in widths] + [heads_spec, heads_spec],
        out_shape=[jax.ShapeDtypeStruct((n, w), t) for w, t in zip(widths, dtypes)]
        + [heads_shape, heads_shape],
        compiler_params=_cparams(("arbitrary",)),
        name="inproj",
    )(xf, mod3, g_pre, w_in_p)


def _attn_kernel(q_ref, kd_ref, vd_ref, kp_ref, vp_ref, o_ref, kbf, vbf, oacc, carry, *,
                 tq, tk, n_prev_static):
    qi = pl.program_id(2)
    lp = kp_ref.shape[0]
    cast_rows = min(lp, 512)

    prev_reused = n_prev_static is None

    if prev_reused:
        @pl.when(qi == 0)
        def _():
            def cp(c, _):
                rows = pl.ds(pl.multiple_of(c * cast_rows, cast_rows), cast_rows)
                for h in range(2):
                    lanes = slice(h * HEAD_DIM, (h + 1) * HEAD_DIM)
                    kbf[h, rows, :] = kp_ref[rows, lanes].astype(BF16)
                    vbf[h, rows, :] = vp_ref[rows, lanes].astype(BF16)
                return 0
            lax.fori_loop(0, lp // cast_rows, cp, 0)

    td = min(tq, tk)
    kk = lax.broadcasted_iota(jnp.int32, (tk, tk), 0)
    nn = lax.broadcasted_iota(jnp.int32, (tk, tk), 1)
    msum = jnp.where(kk > nn, 1.0, 0.0).astype(BF16)

    oacc[...] = jnp.zeros_like(oacc)
    carry[...] = jnp.zeros_like(carry)

    def step(h, rows, z2, v):
        nk = z2.shape[1]
        zb = z2.astype(BF16)
        l2 = jnp.log(1.0 + jnp.exp2(-jnp.abs(zb))) * LOG2E
        soft = jnp.maximum(zb, 0.0) + l2
        log_beta = jnp.minimum(zb, 0.0) - l2
        newer = jnp.dot(soft, msum[:nk, :nk], preferred_element_type=F32)
        c = carry[h, rows, :]
        gone = (newer + jnp.concatenate([c] * (nk // LANES), axis=1)).astype(BF16)
        w = jnp.exp2(log_beta - gone)
        carry[h, rows, :] = c + (newer[:, 0:1] + soft[:, 0:1].astype(F32))
        oacc[h, rows, :] += jnp.dot(w, v, preferred_element_type=F32)

    for jb in reversed(range(tq // td)):
        r0 = jb * td
        rows = slice(r0, tq)
        keys = slice(r0, r0 + td)
        rl = lax.broadcasted_iota(jnp.int32, (tq - r0, td), 0)
        cl = lax.broadcasted_iota(jnp.int32, (tq - r0, td), 1)
        causal = cl < rl
        for h in range(2):
            lanes = slice(h * HEAD_DIM, (h + 1) * HEAD_DIM)
            z2 = lax.dot_general(q_ref[rows, lanes], kd_ref[keys, lanes].astype(BF16), NT_DIMS,
                                 preferred_element_type=F32)
            step(h, rows, jnp.where(causal, z2, MASKED_SCORE), vd_ref[keys, lanes].astype(BF16))

    n_prev = qi * (tq // tk) if n_prev_static is None else n_prev_static
    prev_multiple = (tq // tk) if n_prev_static is None else n_prev_static
    unroll = next(u for u in (ATTN_UNROLL, 2, 1) if prev_multiple % u == 0)

    def body(jj, _):
        for u in range(unroll):
            keys = pl.ds(pl.multiple_of((n_prev - 1 - unroll * jj - u) * tk, tk), tk)
            for h in range(2):
                lanes = slice(h * HEAD_DIM, (h + 1) * HEAD_DIM)
                if prev_reused:
                    kb, vb = kbf[h, keys, :], vbf[h, keys, :]
                else:
                    kb = kp_ref[keys, lanes].astype(BF16)
                    vb = vp_ref[keys, lanes].astype(BF16)
                z2 = lax.dot_general(q_ref[:, lanes], kb, NT_DIMS, preferred_element_type=F32)
                step(h, slice(0, tq), z2, vb)
        return 0

    lax.fori_loop(0, n_prev // unroll, body, 0)
    o_ref[...] = jnp.concatenate([oacc[0], oacc[1]], axis=1)


def _attention(q, k, v, k_prev, v_prev, nb, seq_len, prev_len, prev_is_self, tq, tk):
    n = q.shape[0]
    nq = seq_len // tq
    assert seq_len % tq == 0 and prev_len % tk == 0 and (tq % tk == 0 or tk % tq == 0)
    blk = lambda b, hp, qi: (b * nq + qi, hp)
    prev = lambda b, hp, qi: (b, hp)
    kern = functools.partial(_attn_kernel, tq=tq, tk=tk,
                             n_prev_static=None if prev_is_self else prev_len // tk)
    return pl.pallas_call(
        kern,
        grid=(nb, H_ATT // 2, nq),
        in_specs=[pl.BlockSpec((tq, LANES), blk),
                  pl.BlockSpec((tq, LANES), blk),
                  pl.BlockSpec((tq, LANES), blk),
                  pl.BlockSpec((prev_len, LANES), prev),
                  pl.BlockSpec((prev_len, LANES), prev)],
        out_specs=pl.BlockSpec((tq, LANES), blk),
        out_shape=jax.ShapeDtypeStruct((n, D_ATT), F32),
        scratch_shapes=[pltpu.VMEM((2, prev_len, HEAD_DIM), BF16),
                        pltpu.VMEM((2, prev_len, HEAD_DIM), BF16),
                        pltpu.VMEM((2, tq, HEAD_DIM), F32),
                        pltpu.VMEM((2, tq, LANES), F32)],
        compiler_params=_cparams(("arbitrary", "arbitrary", "arbitrary")),
        name="attn",
    )(q, k, v, k_prev, v_prev)


def _ssd_kernel(xbc_ref, dt_ref, z_ref, conv0_ref, ssm0_ref, cw_ref, cb_ref, dtb_ref, alog_ref,
                dsk_ref, g_ref, y_ref, ssm_ref, buf, state, *, q, valid):
    ci = pl.program_id(1)
    hist = SUBLANES

    @pl.when(ci == 0)
    def _():
        buf[0:hist, :] = conv0_ref[0]
        state[...] = ssm0_ref[0]

    buf[hist:hist + q, :] = xbc_ref[...]
    cw = cw_ref[...]
    acc = cb_ref[...] + cw[CONV_W - 1:CONV_W, :] * buf[hist:hist + q, :]
    for s in range(1, CONV_W):
        acc = acc + cw[CONV_W - 1 - s:CONV_W - s, :] * buf[hist - s:hist - s + q, :]
    xc = acc * _sigmoid(acc)
    buf[0:hist, :] = buf[q:q + hist, :]

    lane = lax.broadcasted_iota(jnp.int32, (q, LANES), 1)
    rowi = lax.broadcasted_iota(jnp.int32, (q, LANES), 0)
    dt = _softplus(dt_ref[...] + dtb_ref[...])
    dt = jnp.where(lane < H_SSD, dt, 0.0)
    if valid < q:
        dt = jnp.where(rowi < valid, dt, 0.0)
    d_a = dt * (-jnp.exp(alog_ref[...]))

    tt = lax.broadcasted_iota(jnp.int32, (q, q), 0)
    ss = lax.broadcasted_iota(jnp.int32, (q, q), 1)
    causal = ss <= tt
    cum = _dot_01(jnp.where(causal, 1.0, 0.0).astype(BF16), d_a)
    cum_t = cum.T
    last = cum[valid - 1:valid, :]
    e_cum = jnp.exp(cum)
    e_end = jnp.exp(last - cum)
    e_last = jnp.exp(last)

    ys = []
    for g in range(SSD_GROUPS):
        b_off = D_SSD + g * SSD_N
        c_off = D_SSD + SSD_GROUPS * SSD_N + g * SSD_N
        bg = xc[:, b_off:b_off + SSD_N].astype(BF16)
        cg = xc[:, c_off:c_off + SSD_N].astype(BF16)
        cb = lax.dot_general(cg, bg, NT_DIMS, preferred_element_type=F32)
        for pair in range(HEADS_PER_GROUP // 2):
            xws = []
            for hh in range(2):
                h = g * HEADS_PER_GROUP + pair * 2 + hh
                xh = xc[:, h * SSD_P:(h + 1) * SSD_P]
                seg = cum[:, h:h + 1] - cum_t[h:h + 1, :]
                decay = jnp.exp(jnp.where(causal, seg, -jnp.inf))
                xdt = xh * dt[:, h:h + 1]
                y_diag = jnp.dot((cb * decay).astype(BF16), xdt.astype(BF16),
                                 preferred_element_type=F32)
                st = state[h]
                y_off = lax.dot_general(cg, st.astype(BF16), NT_DIMS,
                                        preferred_element_type=F32) * e_cum[:, h:h + 1]
                ys.append(y_diag + y_off + dsk_ref[:, h * SSD_P:(h + 1) * SSD_P] * xh)
                xws.append(xdt * e_end[:, h:h + 1])
            xw_t = jnp.concatenate(xws, axis=1).T.astype(BF16)
            for hh in range(2):
                h = g * HEADS_PER_GROUP + pair * 2 + hh
                upd = jnp.dot(xw_t[hh * SSD_P:(hh + 1) * SSD_P, :], bg, preferred_element_type=F32)
                state[h] = state[h] * e_last[:, h:h + 1] + upd
    y = jnp.concatenate(ys, axis=1)
    zz = z_ref[...]
    y_ref[...] = _rms(y * (zz * _sigmoid(zz))) * g_ref[...]

    @pl.when(ci == pl.num_programs(1) - 1)
    def _():
        ssm_ref[0] = state[...]


def _ssd(xbc, dt, z, conv0_p, ssm0, cw, cb, dtb, alog, dsk, g_ssd, nb, seq_len, q, valid):
    n = xbc.shape[0]
    nc = seq_len // q
    row = lambda b, c: (b * nc + c, 0)
    full2 = lambda b, c: (0, 0)
    return pl.pallas_call(
        functools.partial(_ssd_kernel, q=q, valid=valid),
        grid=(nb, nc),
        in_specs=[pl.BlockSpec((q, CONV_DIM), row),
                  pl.BlockSpec((q, LANES), row),
                  pl.BlockSpec((q, D_SSD), row),
                  pl.BlockSpec((1, SUBLANES, CONV_DIM), lambda b, c: (b, 0, 0)),
                  pl.BlockSpec((1, H_SSD, SSD_P, SSD_N), lambda b, c: (b, 0, 0, 0)),
                  pl.BlockSpec((CONV_W, CONV_DIM), full2),
                  pl.BlockSpec((1, CONV_DIM), full2),
                  pl.BlockSpec((1, LANES), full2),
                  pl.BlockSpec((1, LANES), full2),
                  pl.BlockSpec((1, D_SSD), full2),
                  pl.BlockSpec((1, D_SSD), full2)],
        out_specs=[pl.BlockSpec((q, D_SSD), row),
                   pl.BlockSpec((1, H_SSD, SSD_P, SSD_N), lambda b, c: (b, 0, 0, 0))],
        out_shape=[jax.ShapeDtypeStruct((n, D_SSD), F32),
                   jax.ShapeDtypeStruct((nb, H_SSD, SSD_P, SSD_N), F32)],
        scratch_shapes=[pltpu.VMEM((q + SUBLANES, CONV_DIM), F32),
                        pltpu.VMEM((H_SSD, SSD_P, SSD_N), F32)],
        compiler_params=_cparams(("arbitrary", "arbitrary")),
        name="ssd",
    )(xbc, dt, z, conv0_p, ssm0, cw, cb, dtb, alog, dsk, g_ssd)


def _top_values(s, count):
    outs = []
    cur = s
    for r in range(count):
        m = jnp.max(cur, axis=0, keepdims=True)
        outs.append(m)
        if r + 1 < count:
            cur = jnp.where(cur == m, -jnp.inf, cur)
    return outs


def _twin_bf16(x):
    bits = pltpu.bitcast(x.astype(BF16).astype(F32), jnp.uint32)
    return bits | (bits >> 16)


INT32_MIN = -2 ** 31
GATE_STEP = 2.0 ** 64


def _order_key(x):
    bits = pltpu.bitcast(x, jnp.int32)
    return bits ^ ((bits >> 31) & 0x7FFFFFFF)


def _order_key_inverse(key):
    return pltpu.bitcast(key ^ ((key >> 31) & 0x7FFFFFFF), F32)


def _top_ranked(s, count):
    outs = []
    cur = _order_key(s)
    for r in range(count):
        m = jnp.max(cur, axis=0, keepdims=True)
        outs.append(_order_key_inverse(m))
        cur = jnp.where(cur == m, INT32_MIN + r, cur)
    rank = jnp.where(cur < INT32_MIN + count, cur - INT32_MIN, count).astype(F32)
    return outs, rank


def _mid_kernel(x_ref, oa_ref, os_ref, mod_ref, ga_ref, gpm_ref, gpf_ref, wo_ref, wpq_ref, sk_ref,
                x1_ref, h2t_ref, at_ref, cnt_ref, r2_ref, bt_ref, *, d):
    mod = mod_ref[0]
    gt1 = mod[:, 2 * d:3 * d]
    sh2 = mod[:, 3 * d:4 * d]
    sc2 = mod[:, 4 * d:5 * d]
    attn_o = (_rms(oa_ref[...]) * ga_ref[...]).astype(BF16)
    mix = (jnp.dot(attn_o, wo_ref[0:D_ATT, :], preferred_element_type=F32)
           + jnp.dot(os_ref[...].astype(BF16), wo_ref[D_ATT:D_ATT + D_SSD, :],
                     preferred_element_type=F32))
    x1 = x_ref[...] + gt1 * (_rms(mix) * gpm_ref[...])
    x1_ref[...] = x1
    h2 = _rms(x1) * gpf_ref[...] * (1.0 + sc2) + sh2
    h2t_ref[...] = h2.T.astype(BF16)
    qh = jnp.dot(h2.astype(BF16), wpq_ref[...], preferred_element_type=F32).astype(BF16)
    k1 = sk_ref[0]
    k2 = sk_ref[1]
    for h in range(PEER_HEADS):
        base = h * 2 * PEER_HALF
        s1 = lax.dot_general(k1, qh[:, base:base + PEER_HALF], NT_DIMS,
                             preferred_element_type=F32)
        s2 = lax.dot_general(k2, qh[:, base + PEER_HALF:base + 2 * PEER_HALF], NT_DIMS,
                             preferred_element_type=F32)
        t1 = _top_values(s1, PEER_TOPK)
        t2, rank2 = _top_ranked(s2, PEER_TOPK)
        half = PEER_TOPK // 2
        t1_lo = jnp.concatenate(t1[:half], axis=0)
        t1_hi = jnp.concatenate(t1[half:], axis=0)
        t2_hi = jnp.concatenate(t2[half:], axis=0)
        rank_a = lax.broadcasted_iota(jnp.int32, t1_lo.shape, 0)
        blocks = [t1_lo + t2[0], t1_hi + t2[0], t1[0] + t2_hi]
        for b in range(1, half):
            blocks.append(jnp.where(rank_a < PEER_TOPK // (b + 1), t1_lo + t2[b], -jnp.inf))
        best = _top_values(jnp.concatenate(blocks, axis=0), PEER_TOPK)
        tau = best[PEER_TOPK - 1]
        zsum = jnp.ones_like(best[0])
        for r in range(1, PEER_TOPK):
            zsum = zsum + jnp.exp(best[r] - best[0])
        taken = [jnp.where(blk >= tau, 1.0, 0.0) for blk in blocks]
        cnt_lo = taken[0]
        for blk in taken[3:]:
            cnt_lo = cnt_lo + blk
        a0_extra = jnp.sum(taken[2], axis=0, keepdims=True)
        cnt_lo = cnt_lo + jnp.where(rank_a == 0, a0_extra, 0.0)
        cnt_by_rank = jnp.concatenate([cnt_lo, taken[1]], axis=0)
        cnt = jnp.zeros_like(s1)
        for a in range(PEER_TOPK):
            cnt = jnp.where(s1 == t1[a], cnt_by_rank[a:a + 1, :], cnt)
        at_ref[h] = _twin_bf16(jnp.exp(s1 - t1[0]) * (0.5 / zsum))
        cnt_ref[h] = _twin_bf16(cnt * GATE_STEP)
        r2_ref[h] = rank2 * GATE_STEP
        bt_ref[h] = jnp.exp(s2 - t2[0])


def _mid(xf, o_att, o_ssd, mod3, g_attn, g_post_mix, g_pre_ffn, w_out, w_pq, sub_keys, seq_len, tm):
    n, d = xf.shape
    tiles_per_seq = seq_len // tm
    row = lambda i: (i, 0)
    full = lambda i: (0, 0)
    colblk = lambda i: (0, 0, i)
    gate_u32 = jax.ShapeDtypeStruct((PEER_HEADS, N_KEYS, n), jnp.uint32)
    gate_f32 = jax.ShapeDtypeStruct((PEER_HEADS, N_KEYS, n), F32)
    gate_spec = pl.BlockSpec((PEER_HEADS, N_KEYS, tm), colblk)
    return pl.pallas_call(
        functools.partial(_mid_kernel, d=d),
        grid=(n // tm,),
        in_specs=[pl.BlockSpec((tm, d), row),
                  pl.BlockSpec((tm, D_ATT), row),
                  pl.BlockSpec((tm, D_SSD), row),
                  pl.BlockSpec((1, 1, 6 * d), lambda i: (i // tiles_per_seq, 0, 0)),
                  pl.BlockSpec((1, D_ATT), full),
                  pl.BlockSpec((1, d), full),
                  pl.BlockSpec((1, d), full),
                  pl.BlockSpec(w_out.shape, full),
                  pl.BlockSpec(w_pq.shape, full),
                  pl.BlockSpec(sub_keys.shape, lambda i: (0, 0, 0))],
        out_specs=[pl.BlockSpec((tm, d), row),
                   pl.BlockSpec((d, tm), lambda i: (0, i)),
                   gate_spec, gate_spec, gate_spec, gate_spec],
        out_shape=[jax.ShapeDtypeStruct((n, d), F32),
                   jax.ShapeDtypeStruct((d, n), BF16),
                   gate_u32, gate_u32, gate_f32, gate_f32],
        compiler_params=_cparams(("arbitrary",)),
        name="mid",
    )(xf, o_att, o_ssd, mod3, g_attn, g_post_mix, g_pre_ffn, w_out, w_pq, sub_keys)


GELU_C = 0.7978845608028654
PEER_SUB = 256
PEER_SUBS_PER_ACC = 2
PEER_TE = 2048


def _twin_rows(word_row, rows):
    tile = pltpu.bitcast(jnp.broadcast_to(word_row, (SUBLANES, word_row.shape[1])), BF16)
    return jnp.concatenate([tile] * (rows // PACKED_ROWS), axis=0)


def _gelu_tanh_x2(x):
    return x * (1.0 + jnp.tanh(x * ((x * x) * (GELU_C * 0.044715) + GELU_C)))


def _peer_kernel(h2t_ref, u_ref, vt_ref, at_ref, cnt_ref, r2_ref, bt_ref, x1_ref, mod_ref, g_ref,
                 o_ref, acc, r2s, bts, *, d, tm, te, seq_rows):
    j = pl.program_id(1)

    @pl.when(j == 0)
    def _():
        acc[...] = jnp.zeros_like(acc)
        for h in range(PEER_HEADS):
            r2s[h] = r2_ref[h].astype(BF16)
            bts[h] = bt_ref[h].astype(BF16)

    i_per_sub = PEER_SUB // N_KEYS
    n_sub = te // PEER_SUB

    def scores(s):
        return jnp.dot(u_ref[s * PEER_SUB:(s + 1) * PEER_SUB, :], h2t_ref[...],
                       preferred_element_type=F32)

    sc = scores(0)
    pending = []
    for s in range(n_sub):
        sc_next = scores(s + 1) if s + 1 < n_sub else None
        act = _gelu_tanh_x2(sc.astype(BF16))
        lane_blocks = []
        for lg in range(tm // LANES):
            lanes = slice(lg * LANES, (lg + 1) * LANES)
            gsums = [None] * i_per_sub
            for h in range(PEER_HEADS):
                r2 = r2s[h, :, lanes]
                b = bts[h, :, lanes]
                for il in range(i_per_sub):
                    ii = s * i_per_sub + il
                    cnt = _twin_rows(cnt_ref[h, ii:ii + 1, lanes], N_KEYS)
                    a = _twin_rows(at_ref[h, ii:ii + 1, lanes], N_KEYS)
                    term = jnp.minimum(b * a, jnp.maximum(cnt - r2, 0.0))
                    gsums[il] = term if gsums[il] is None else gsums[il] + term
            lane_blocks.append(jnp.concatenate(
                [act[il * N_KEYS:(il + 1) * N_KEYS, lanes] * gsums[il] for il in range(i_per_sub)],
                axis=0))
        pending.append(jnp.concatenate(lane_blocks, axis=1))
        if len(pending) == PEER_SUBS_PER_ACC or s + 1 == n_sub:
            first = s + 1 - len(pending)
            acc[...] += jnp.dot(vt_ref[:, first * PEER_SUB:(s + 1) * PEER_SUB],
                                jnp.concatenate(pending, axis=0), preferred_element_type=F32)
            pending = []
        sc = sc_next

    @pl.when(j == pl.num_programs(1) - 1)
    def _():
        f = _rms(acc[...].T) * g_ref[...]
        for s in range(tm // seq_rows):
            rows = slice(s * seq_rows, (s + 1) * seq_rows)
            gt2 = mod_ref[s][:, 5 * d:6 * d]
            o_ref[rows, :] = x1_ref[rows, :] + gt2 * f[rows, :]


def _peer(h2t, u_bf, vt_bf, at, cnt, r2, bt, x1, mod3, g_post_ffn, seq_len, tm, te):
    d, n = h2t.shape
    n_exp = u_bf.shape[0]
    seq_rows = min(seq_len, tm)
    seqs_per_tile = tm // seq_rows
    gate_spec = pl.BlockSpec((PEER_HEADS, N_KEYS, tm), lambda i, j: (0, 0, i))
    irow_spec = pl.BlockSpec((PEER_HEADS, te // N_KEYS, tm), lambda i, j: (0, j, i))
    assert (te // N_KEYS) % SUBLANES == 0
    if seqs_per_tile == 1:
        tiles_per_seq = seq_len // tm
        mod_spec = pl.BlockSpec((1, 1, 6 * d), lambda i, j: (i // tiles_per_seq, 0, 0))
    else:
        mod_spec = pl.BlockSpec((seqs_per_tile, 1, 6 * d), lambda i, j: (i, 0, 0))
    return pl.pallas_call(
        functools.partial(_peer_kernel, d=d, tm=tm, te=te, seq_rows=seq_rows),
        grid=(n // tm, n_exp // te),
        in_specs=[pl.BlockSpec((d, tm), lambda i, j: (0, i)),
                  pl.BlockSpec((te, d), lambda i, j: (j, 0)),
                  pl.BlockSpec((d, te), lambda i, j: (0, j)),
                  irow_spec, irow_spec, gate_spec, gate_spec,
                  pl.BlockSpec((tm, d), lambda i, j: (i, 0)),
                  mod_spec,
                  pl.BlockSpec((1, d), lambda i, j: (0, 0))],
        out_specs=pl.BlockSpec((tm, d), lambda i, j: (i, 0)),
        out_shape=jax.ShapeDtypeStruct((n, d), F32),
        scratch_shapes=[pltpu.VMEM((d, tm), F32),
                        pltpu.VMEM((PEER_HEADS, N_KEYS, tm), BF16),
                        pltpu.VMEM((PEER_HEADS, N_KEYS, tm), BF16)],
        compiler_params=_cparams(("arbitrary", "arbitrary")),
        name="peer",
    )(h2t, u_bf, vt_bf, at, cnt, r2, bt, x1, mod3, g_post_ffn)


def _pick_tile(seq_len, target):
    t = min(seq_len, target)
    assert seq_len % t == 0 and t % ROW_TILE == 0
    return t


def _layer(x, mod, k_past, v_past, ssm0, conv0, valid, wts):
    nb, seq_len, d = x.shape
    n = nb * seq_len
    xf = x.reshape(n, d)
    mod3 = mod.reshape(nb, 1, 6 * d)

    q, k, v, z, xbc, dt, k_heads, v_heads = _inproj(xf, mod3, wts["g_pre_mix"], wts["w_in"], seq_len,
                                  _pick_tile(seq_len, INPROJ_TM))
    tq = _pick_tile(seq_len, ATTN_TQ)
    if k_past is None:
        o_att = _attention(q, k, v, k, v, nb, seq_len, seq_len, True, tq, min(tq, ATTN_TK))
    else:
        past = k_past.shape[1]
        o_att = _attention(q, k, v, k_past.reshape(nb * past, D_ATT), v_past.reshape(nb * past, D_ATT),
                           nb, seq_len, past, False, tq, ATTN_TK)

    conv0_p = jnp.pad(conv0, ((0, 0), (SUBLANES - (CONV_W - 1), 0), (0, 0)))
    chunk = _pick_tile(seq_len, SSD_CHUNK)
    assert valid == seq_len or seq_len == chunk
    o_ssd, ssm_new = _ssd(xbc, dt, z, conv0_p, ssm0, wts["conv_w"], wts["conv_b"], wts["dt_bias"],
                          wts["a_log"], wts["d_skip"], wts["g_ssd_norm"], nb, seq_len, chunk,
                          min(valid, chunk))

    x1, h2t, at, cnt, r2, bt = _mid(xf, o_att, o_ssd, mod3, wts["g_attn_norm"], wts["g_post_mix"],
                                    wts["g_pre_ffn"], wts["w_out"], wts["w_pq"], wts["sub_keys"],
                                    seq_len, _pick_tile(seq_len, MID_TM))
    spans_ok = seq_len % PEER_TM == 0 or PEER_TM % seq_len == 0
    tm = PEER_TM if n % PEER_TM == 0 and spans_ok else seq_len
    y = _peer(h2t, wts["u_tab"], wts["vt_tab"], at, cnt, r2, bt, x1, mod3, wts["g_post_ffn"],
              seq_len, tm, PEER_TE)

    assert valid >= CONV_W - 1
    conv_new = xbc.reshape(nb, seq_len, CONV_DIM)[:, valid - (CONV_W - 1):valid]
    return (y.reshape(nb, seq_len, d)[:, :valid],
            k_heads.reshape(nb, seq_len, H_ATT, HEAD_DIM)[:, :valid],
            v_heads.reshape(nb, seq_len, H_ATT, HEAD_DIM)[:, :valid],
            ssm_new, conv_new)


def _pad_lanes(a, width):
    return jnp.pad(a, ((0, 0), (0, width - a.shape[1])))


def kernel(x_prompt, x_sample, c_prompt, c_sample, cache_k, cache_v, state_ssm, state_conv, w_ada, b_ada, g_pre_mix, g_post_mix, g_pre_ffn, g_post_ffn, w_in, conv_w, conv_b, dt_bias, a_log, d_skip, g_attn_norm, g_ssd_norm, w_out, w_pq, sub_keys, u_tab, v_tab):
    depth = w_ada.shape[0]
    bp, _, d = x_prompt.shape
    bs, dec_len, _ = x_sample.shape
    d_in = w_in.shape[2]
    d_in_p = -(-d_in // LANES) * LANES
    pad_len = -(-dec_len // ROW_TILE) * ROW_TILE

    xp = x_prompt
    xq = jnp.pad(x_sample, ((0, 0), (0, pad_len - dec_len), (0, 0)))
    outs = [[] for _ in range(8)]
    for i in range(depth):
        wts = dict(
            g_pre_mix=g_pre_mix[i][None], g_post_mix=g_post_mix[i][None],
            g_pre_ffn=g_pre_ffn[i][None], g_post_ffn=g_post_ffn[i][None],
            w_in=_pad_lanes(w_in[i], d_in_p).astype(BF16),
            conv_w=conv_w[i].reshape(CONV_W, CONV_DIM), conv_b=conv_b[i][None],
            dt_bias=_pad_lanes(dt_bias[i][None], LANES), a_log=_pad_lanes(a_log[i][None], LANES),
            d_skip=jnp.repeat(d_skip[i], SSD_P)[None],
            g_attn_norm=g_attn_norm[i][None], g_ssd_norm=g_ssd_norm[i][None],
            w_out=w_out[i].astype(BF16), w_pq=w_pq[i].astype(BF16),
            sub_keys=sub_keys[i].astype(BF16),
            u_tab=u_tab[i].astype(BF16), vt_tab=v_tab[i].T.astype(BF16),
        )
        mod = _ada(jnp.concatenate([c_prompt, c_sample], axis=0), w_ada[i], b_ada[i])
        ssm_zero = jnp.zeros((bp, H_SSD, SSD_P, SSD_N), F32)
        conv_zero = jnp.zeros((bp, CONV_W - 1, CONV_DIM), F32)
        xp, k1, v1, s1, c1 = _layer(xp, mod[:bp], None, None, ssm_zero, conv_zero,
                                    xp.shape[1], wts)
        xq_full, k2, v2, s2, c2 = _layer(xq, mod[bp:], cache_k[i], cache_v[i], state_ssm[i],
                                         state_conv[i], dec_len, wts)
        xq = jnp.pad(xq_full, ((0, 0), (0, pad_len - dec_len), (0, 0))) if i + 1 < depth else xq_full
        for lst, val in zip(outs, (k1, v1, s1, c1, k2, v2, s2, c2)):
            lst.append(val)
    return (xp, xq) + tuple(jnp.stack(o) for o in outs)
```

```python
import functools

import jax
import jax.numpy as jnp
from jax import lax
from jax.experimental import pallas as pl
from jax.experimental.pallas import tpu as pltpu

F32 = jnp.float32
BF16 = jnp.bfloat16
EPS = 1e-6

H_ATT = 8
HEAD_DIM = 64
D_ATT = H_ATT * HEAD_DIM
H_SSD = 8
SSD_P = 64
D_SSD = H_SSD * SSD_P
SSD_GROUPS = 2
HEADS_PER_GROUP = H_SSD // SSD_GROUPS
SSD_N = 64
CONV_W = 4
CONV_DIM = D_SSD + 2 * SSD_GROUPS * SSD_N
N_KEYS = 128
PEER_HEADS = 8
PEER_TOPK = 16
PEER_HALF = 64

LANES = 128
SUBLANES = 8
PACKED_ROWS = 2 * SUBLANES
ROW_TILE = 128
VMEM_LIMIT = 56 * 1024 * 1024

NT_DIMS = (((1,), (1,)), ((), ()))
LOG2E = 1.4426950408889634
MASKED_SCORE = -1e30
ATTN_TQ = 1024
ATTN_TK = 256
ATTN_UNROLL = 4
SSD_CHUNK = 256
INPROJ_TM = 512
MID_TM = 256
PEER_TM = 512


def _cparams(sem):
    return pltpu.CompilerParams(dimension_semantics=sem, vmem_limit_bytes=VMEM_LIMIT)


def _split3(a):
    hi = a.astype(BF16)
    r = a - hi.astype(F32)
    mid = r.astype(BF16)
    lo = (r - mid.astype(F32)).astype(BF16)
    return hi, mid, lo


def _dot_f32(a, b):
    a0, a1, a2 = _split3(a)
    b0, b1, b2 = _split3(b)
    d = functools.partial(jnp.dot, preferred_element_type=F32)
    return (d(a0, b0) + (d(a0, b1) + d(a1, b0))
            + (d(a0, b2) + d(a1, b1) + d(a2, b0)))


def _dot_01(m01, x):
    x0, x1, x2 = _split3(x)
    d = functools.partial(jnp.dot, preferred_element_type=F32)
    return d(m01, x0) + d(m01, x1) + d(m01, x2)


def _sigmoid(x):
    return 1.0 / (1.0 + jnp.exp(-x))


def _softplus(x):
    return jnp.maximum(x, 0.0) + jnp.log(1.0 + jnp.exp(-jnp.abs(x)))


def _rms(x):
    return x * lax.rsqrt(jnp.mean(x * x, axis=-1, keepdims=True) + EPS)


def _ada_kernel(c_ref, w_ref, b_ref, o_ref):
    c = c_ref[...]
    o_ref[...] = _dot_f32(c * _sigmoid(c), w_ref[...]) + b_ref[...]


def _ada(c, w_ada, b_ada):
    nb, d = c.shape
    n_out = w_ada.shape[1]
    bn = d
    return pl.pallas_call(
        _ada_kernel,
        grid=(n_out // bn,),
        in_specs=[pl.BlockSpec((nb, d), lambda j: (0, 0)),
                  pl.BlockSpec((d, bn), lambda j: (0, j)),
                  pl.BlockSpec((1, bn), lambda j: (0, j))],
        out_specs=pl.BlockSpec((nb, bn), lambda j: (0, j)),
        out_shape=jax.ShapeDtypeStruct((nb, n_out), F32),
        compiler_params=_cparams(("arbitrary",)),
        name="ada",
    )(c, w_ada, b_ada.reshape(1, n_out))


def _inproj_kernel(x_ref, mod_ref, g_ref, w_ref, q_ref, k_ref, v_ref, z_ref, xbc_ref, dt_ref,
                   kh_ref, vh_ref, *, d):
    mod = mod_ref[0]
    sh1 = mod[:, 0:d]
    sc1 = mod[:, d:2 * d]
    h = (_rms(x_ref[...]) * g_ref[...] * (1.0 + sc1) + sh1).astype(BF16)

    def proj(lo, hi):
        return jnp.dot(h, w_ref[:, lo:hi], preferred_element_type=F32)

    o = 0
    q_ref[...] = (proj(o, o + D_ATT) * (HEAD_DIM ** -0.5 * LOG2E)).astype(BF16)
    o += D_ATT
    for dense_ref, heads_ref in ((k_ref, kh_ref), (v_ref, vh_ref)):
        p = proj(o, o + D_ATT)
        dense_ref[...] = p
        for hd in range(H_ATT):
            heads_ref[pl.ds(hd, p.shape[0], stride=H_ATT), :] = p[:, hd * HEAD_DIM:(hd + 1) * HEAD_DIM]
        o += D_ATT
    z_ref[...] = proj(o, o + D_SSD)
    o += D_SSD
    xbc_ref[...] = proj(o, o + CONV_DIM)
    o += CONV_DIM
    dt_ref[...] = proj(o, o + LANES)


def _inproj(xf, mod3, g_pre, w_in_p, seq_len, tm):
    n, d = xf.shape
    tiles_per_seq = seq_len // tm
    row = lambda i: (i, 0)
    full = lambda i: (0, 0)
    widths = (D_ATT, D_ATT, D_ATT, D_SSD, CONV_DIM, LANES)
    dtypes = (BF16, F32, F32, F32, F32, F32)
    heads_spec = pl.BlockSpec((tm * H_ATT, HEAD_DIM), row)
    heads_shape = jax.ShapeDtypeStruct((n * H_ATT, HEAD_DIM), F32)
    return pl.pallas_call(
        functools.partial(_inproj_kernel, d=d),
        grid=(n // tm,),
        in_specs=[pl.BlockSpec((tm, d), row),
                  pl.BlockSpec((1, 1, 6 * d), lambda i: (i // tiles_per_seq, 0, 0)),
                  pl.BlockSpec((1, d), full),
                  pl.BlockSpec(w_in_p.shape, full)],
        out_specs=[pl.BlockSpec((tm, w), row) for w in widths] + [heads_spec, heads_spec],
        out_shape=[jax.ShapeDtypeStruct((n, w), t) for w, t in zip(widths, dtypes)]
        + [heads_shape, heads_shape],
        compiler_params=_cparams(("arbitrary",)),
        name="inproj",
    )(xf, mod3, g_pre, w_in_p)


def _attn_kernel(q_ref, kd_ref, vd_ref, kp_ref, vp_ref, o_ref, kbf, vbf, oacc, carry, *,
                 tq, tk, n_prev_static):
    qi = pl.program_id(2)
    lp = kp_ref.shape[0]
    cast_rows = min(lp, 512)

    prev_reused = n_prev_static is None

    if prev_reused:
        @pl.when(qi == 0)
        def _():
            def cp(c, _):
                rows = pl.ds(pl.multiple_of(c * cast_rows, cast_rows), cast_rows)
                for h in range(2):
                    lanes = slice(h * HEAD_DIM, (h + 1) * HEAD_DIM)
                    kbf[h, rows, :] = kp_ref[rows, lanes].astype(BF16)
                    vbf[h, rows, :] = vp_ref[rows, lanes].astype(BF16)
                return 0
            lax.fori_loop(0, lp // cast_rows, cp, 0)

    td = min(tq, tk)
    kk = lax.broadcasted_iota(jnp.int32, (tk, tk), 0)
    nn = lax.broadcasted_iota(jnp.int32, (tk, tk), 1)
    msum = jnp.where(kk > nn, 1.0, 0.0).astype(BF16)

    oacc[...] = jnp.zeros_like(oacc)
    carry[...] = jnp.zeros_like(carry)

    def step(h, rows, z2, v):
        nk = z2.shape[1]
        zb = z2.astype(BF16)
        l2 = jnp.log(1.0 + jnp.exp2(-jnp.abs(zb))) * LOG2E
        soft = jnp.maximum(zb, 0.0) + l2
        log_beta = jnp.minimum(zb, 0.0) - l2
        newer = jnp.dot(soft, msum[:nk, :nk], preferred_element_type=F32)
        c = carry[h, rows, :]
        gone = (newer + jnp.concatenate([c] * (nk // LANES), axis=1)).astype(BF16)
        w = jnp.exp2(log_beta - gone)
        carry[h, rows, :] = c + (newer[:, 0:1] + soft[:, 0:1].astype(F32))
        oacc[h, rows, :] += jnp.dot(w, v, preferred_element_type=F32)

    for jb in reversed(range(tq // td)):
        r0 = jb * td
        rows = slice(r0, tq)
        keys = slice(r0, r0 + td)
        rl = lax.broadcasted_iota(jnp.int32, (tq - r0, td), 0)
        cl = lax.broadcasted_iota(jnp.int32, (tq - r0, td), 1)
        causal = cl < rl
        for h in range(2):
            lanes = slice(h * HEAD_DIM, (h + 1) * HEAD_DIM)
            z2 = lax.dot_general(q_ref[rows, lanes], kd_ref[keys, lanes].astype(BF16), NT_DIMS,
                                 preferred_element_type=F32)
            step(h, rows, jnp.where(causal, z2, MASKED_SCORE), vd_ref[keys, lanes].astype(BF16))

    n_prev = qi * (tq // tk) if n_prev_static is None else n_prev_static
    prev_multiple = (tq // tk) if n_prev_static is None else n_prev_static
    unroll = next(u for u in (ATTN_UNROLL, 2, 1) if prev_multiple % u == 0)

    def body(jj, _):
        for u in range(unroll):
            keys = pl.ds(pl.multiple_of((n_prev - 1 - unroll * jj - u) * tk, tk), tk)
            for h in range(2):
                lanes = slice(h * HEAD_DIM, (h + 1) * HEAD_DIM)
                if prev_reused:
                    kb, vb = kbf[h, keys, :], vbf[h, keys, :]
                else:
                    kb = kp_ref[keys, lanes].astype(BF16)
                    vb = vp_ref[keys, lanes].astype(BF16)
                z2 = lax.dot_general(q_ref[:, lanes], kb, NT_DIMS, preferred_element_type=F32)
                step(h, slice(0, tq), z2, vb)
        return 0

    lax.fori_loop(0, n_prev // unroll, body, 0)
    o_ref[...] = jnp.concatenate([oacc[0], oacc[1]], axis=1)


def _attention(q, k, v, k_prev, v_prev, nb, seq_len, prev_len, prev_is_self, tq, tk):
    n = q.shape[0]
    nq = seq_len // tq
    assert seq_len % tq == 0 and prev_len % tk == 0 and (tq % tk == 0 or tk % tq == 0)
    blk = lambda b, hp, qi: (b * nq + qi, hp)
    prev = lambda b, hp, qi: (b, hp)
    kern = functools.partial(_attn_kernel, tq=tq, tk=tk,
                             n_prev_static=None if prev_is_self else prev_len // tk)
    return pl.pallas_call(
        kern,
        grid=(nb, H_ATT // 2, nq),
        in_specs=[pl.BlockSpec((tq, LANES), blk),
                  pl.BlockSpec((tq, LANES), blk),
                  pl.BlockSpec((tq, LANES), blk),
                  pl.BlockSpec((prev_len, LANES), prev),
                  pl.BlockSpec((prev_len, LANES), prev)],
        out_specs=pl.BlockSpec((tq, LANES), blk),
        out_shape=jax.ShapeDtypeStruct((n, D_ATT), F32),
        scratch_shapes=[pltpu.VMEM((2, prev_len, HEAD_DIM), BF16),
                        pltpu.VMEM((2, prev_len, HEAD_DIM), BF16),
                        pltpu.VMEM((2, tq, HEAD_DIM), F32),
                        pltpu.VMEM((2, tq, LANES), F32)],
        compiler_params=_cparams(("arbitrary", "arbitrary", "arbitrary")),
        name="attn",
    )(q, k, v, k_prev, v_prev)


def _ssd_kernel(xbc_ref, dt_ref, z_ref, conv0_ref, ssm0_ref, cw_ref, cb_ref, dtb_ref, alog_ref,
                dsk_ref, g_ref, y_ref, ssm_ref, buf, state, *, q, valid):
    ci = pl.program_id(1)
    hist = SUBLANES

    @pl.when(ci == 0)
    def _():
        buf[0:hist, :] = conv0_ref[0]
        state[...] = ssm0_ref[0]

    buf[hist:hist + q, :] = xbc_ref[...]
    cw = cw_ref[...]
    acc = cb_ref[...] + cw[CONV_W - 1:CONV_W, :] * buf[hist:hist + q, :]
    for s in range(1, CONV_W):
        acc = acc + cw[CONV_W - 1 - s:CONV_W - s, :] * buf[hist - s:hist - s + q, :]
    xc = acc * _sigmoid(acc)
    buf[0:hist, :] = buf[q:q + hist, :]

    lane = lax.broadcasted_iota(jnp.int32, (q, LANES), 1)
    rowi = lax.broadcasted_iota(jnp.int32, (q, LANES), 0)
    dt = _softplus(dt_ref[...] + dtb_ref[...])
    dt = jnp.where(lane < H_SSD, dt, 0.0)
    if valid < q:
        dt = jnp.where(rowi < valid, dt, 0.0)
    d_a = dt * (-jnp.exp(alog_ref[...]))

    tt = lax.broadcasted_iota(jnp.int32, (q, q), 0)
    ss = lax.broadcasted_iota(jnp.int32, (q, q), 1)
    causal = ss <= tt
    cum = _dot_01(jnp.where(causal, 1.0, 0.0).astype(BF16), d_a)
    cum_t = cum.T
    last = cum[valid - 1:valid, :]
    e_cum = jnp.exp(cum)
    e_end = jnp.exp(last - cum)
    e_last = jnp.exp(last)

    ys = []
    for g in range(SSD_GROUPS):
        b_off = D_SSD + g * SSD_N
        c_off = D_SSD + SSD_GROUPS * SSD_N + g * SSD_N
        bg = xc[:, b_off:b_off + SSD_N].astype(BF16)
        cg = xc[:, c_off:c_off + SSD_N].astype(BF16)
        cb = lax.dot_general(cg, bg, NT_DIMS, preferred_element_type=F32)
        for pair in range(HEADS_PER_GROUP // 2):
            xws = []
            for hh in range(2):
                h = g * HEADS_PER_GROUP + pair * 2 + hh
                xh = xc[:, h * SSD_P:(h + 1) * SSD_P]
                seg = cum[:, h:h + 1] - cum_t[h:h + 1, :]
                decay = jnp.exp(jnp.where(causal, seg, -jnp.inf))
                xdt = xh * dt[:, h:h + 1]
                y_diag = jnp.dot((cb * decay).astype(BF16), xdt.astype(BF16),
                                 preferred_element_type=F32)
                st = state[h]
                y_off = lax.dot_general(cg, st.astype(BF16), NT_DIMS,
                                        preferred_element_type=F32) * e_cum[:, h:h + 1]
                ys.append(y_diag + y_off + dsk_ref[:, h * SSD_P:(h + 1) * SSD_P] * xh)
                xws.append(xdt * e_end[:, h:h + 1])
            xw_t = jnp.concatenate(xws, axis=1).T.astype(BF16)
            for hh in range(2):
                h = g * HEADS_PER_GROUP + pair * 2 + hh
                upd = jnp.dot(xw_t[hh * SSD_P:(hh + 1) * SSD_P, :], bg, preferred_element_type=F32)
                state[h] = state[h] * e_last[:, h:h + 1] + upd
    y = jnp.concatenate(ys, axis=1)
    zz = z_ref[...]
    y_ref[...] = _rms(y * (zz * _sigmoid(zz))) * g_ref[...]

    @pl.when(ci == pl.num_programs(1) - 1)
    def _():
        ssm_ref[0] = state[...]


def _ssd(xbc, dt, z, conv0_p, ssm0, cw, cb, dtb, alog, dsk, g_ssd, nb, seq_len, q, valid):
    n = xbc.shape[0]
    nc = seq_len // q
    row = lambda b, c: (b * nc + c, 0)
    full2 = lambda b, c: (0, 0)
    return pl.pallas_call(
        functools.partial(_ssd_kernel, q=q, valid=valid),
        grid=(nb, nc),
        in_specs=[pl.BlockSpec((q, CONV_DIM), row),
                  pl.BlockSpec((q, LANES), row),
                  pl.BlockSpec((q, D_SSD), row),
                  pl.BlockSpec((1, SUBLANES, CONV_DIM), lambda b, c: (b, 0, 0)),
                  pl.BlockSpec((1, H_SSD, SSD_P, SSD_N), lambda b, c: (b, 0, 0, 0)),
                  pl.BlockSpec((CONV_W, CONV_DIM), full2),
                  pl.BlockSpec((1, CONV_DIM), full2),
                  pl.BlockSpec((1, LANES), full2),
                  pl.BlockSpec((1, LANES), full2),
                  pl.BlockSpec((1, D_SSD), full2),
                  pl.BlockSpec((1, D_SSD), full2)],
        out_specs=[pl.BlockSpec((q, D_SSD), row),
                   pl.BlockSpec((1, H_SSD, SSD_P, SSD_N), lambda b, c: (b, 0, 0, 0))],
        out_shape=[jax.ShapeDtypeStruct((n, D_SSD), F32),
                   jax.ShapeDtypeStruct((nb, H_SSD, SSD_P, SSD_N), F32)],
        scratch_shapes=[pltpu.VMEM((q + SUBLANES, CONV_DIM), F32),
                        pltpu.VMEM((H_SSD, SSD_P, SSD_N), F32)],
        compiler_params=_cparams(("arbitrary", "arbitrary")),
        name="ssd",
    )(xbc, dt, z, conv0_p, ssm0, cw, cb, dtb, alog, dsk, g_ssd)


def _twin_bf16(x):
    bits = pltpu.bitcast(x.astype(BF16).astype(F32), jnp.uint32)
    return bits | (bits >> 16)


GATE_STEP = 2.0 ** 64


def _merge_exchange_network(n):
    t = (n - 1).bit_length()
    p = 1 << (t - 1)
    pairs = []
    while p > 0:
        q, r, d = 1 << (t - 1), 0, p
        while True:
            pairs += [(i, i + d) for i in range(n - d) if (i & p) == r]
            if q == p:
                break
            d, q, r = q - p, q >> 1, p
        p >>= 1
    return pairs


def _sorted_top(s, count):
    assert s.shape[0] == count * SUBLANES and count & (count - 1) == 0
    slabs = [s[k * SUBLANES:(k + 1) * SUBLANES, :] for k in range(count)]
    for i, j in _merge_exchange_network(count):
        slabs[i], slabs[j] = jnp.maximum(slabs[i], slabs[j]), jnp.minimum(slabs[i], slabs[j])
    shift = SUBLANES // 2
    while shift >= 1:
        partner = [pltpu.roll(x, shift, axis=0) for x in slabs]
        slabs = [jnp.maximum(slabs[i], partner[count - 1 - i]) for i in range(count)]
        stride = count // 2
        while stride >= 1:
            for i in range(count):
                if (i // stride) % 2 == 0:
                    j = i + stride
                    slabs[i], slabs[j] = (jnp.maximum(slabs[i], slabs[j]),
                                          jnp.minimum(slabs[i], slabs[j]))
            stride //= 2
        shift //= 2
    return [x[0:1, :] for x in slabs]


def _mid_kernel(x_ref, oa_ref, os_ref, mod_ref, ga_ref, gpm_ref, gpf_ref, wo_ref, wpq_ref, sk_ref,
                x1_ref, h2t_ref, at_ref, cnt_ref, r2_ref, bt_ref, *, d):
    mod = mod_ref[0]
    gt1 = mod[:, 2 * d:3 * d]
    sh2 = mod[:, 3 * d:4 * d]
    sc2 = mod[:, 4 * d:5 * d]
    attn_o = (_rms(oa_ref[...]) * ga_ref[...]).astype(BF16)
    mix = (jnp.dot(attn_o, wo_ref[0:D_ATT, :], preferred_element_type=F32)
           + jnp.dot(os_ref[...].astype(BF16), wo_ref[D_ATT:D_ATT + D_SSD, :],
                     preferred_element_type=F32))
    x1 = x_ref[...] + gt1 * (_rms(mix) * gpm_ref[...])
    x1_ref[...] = x1
    h2 = _rms(x1) * gpf_ref[...] * (1.0 + sc2) + sh2
    h2t_ref[...] = h2.T.astype(BF16)
    qh = jnp.dot(h2.astype(BF16), wpq_ref[...], preferred_element_type=F32).astype(BF16)
    k1 = sk_ref[0]
    k2 = sk_ref[1]
    for h in range(PEER_HEADS):
        base = h * 2 * PEER_HALF
        s1 = lax.dot_general(k1, qh[:, base:base + PEER_HALF], NT_DIMS,
                             preferred_element_type=F32)
        s2 = lax.dot_general(k2, qh[:, base + PEER_HALF:base + 2 * PEER_HALF], NT_DIMS,
                             preferred_element_type=F32)
        t1 = _sorted_top(s1, PEER_TOPK)
        t2 = _sorted_top(s2, PEER_TOPK)
        rank2 = jnp.zeros_like(s2)
        for b in range(PEER_TOPK):
            rank2 = jnp.where(t2[b] > s2, float(b + 1), rank2)
        half = PEER_TOPK // 2
        t1_lo = jnp.concatenate(t1[:half], axis=0)
        t1_hi = jnp.concatenate(t1[half:], axis=0)
        t2_hi = jnp.concatenate(t2[half:], axis=0)
        rank_a = lax.broadcasted_iota(jnp.int32, t1_lo.shape, 0)
        blocks = [t1_lo + t2[0], t1_hi + t2[0], t1[0] + t2_hi]
        for b in range(1, half):
            blocks.append(jnp.where(rank_a < PEER_TOPK // (b + 1), t1_lo + t2[b], -jnp.inf))
        filler = [jnp.full_like(t1_lo, -jnp.inf)] * (PEER_TOPK - len(blocks))
        best = _sorted_top(jnp.concatenate(blocks + filler, axis=0), PEER_TOPK)
        tau = best[PEER_TOPK - 1]
        zsum = jnp.ones_like(best[0])
        for r in range(1, PEER_TOPK):
            zsum = zsum + jnp.exp(best[r] - best[0])
        taken = [jnp.where(blk >= tau, 1.0, 0.0) for blk in blocks]
        cnt_lo = taken[0]
        for blk in taken[3:]:
            cnt_lo = cnt_lo + blk
        a0_extra = jnp.sum(taken[2], axis=0, keepdims=True)
        cnt_lo = cnt_lo + jnp.where(rank_a == 0, a0_extra, 0.0)
        cnt_by_rank = jnp.concatenate([cnt_lo, taken[1]], axis=0)
        cnt = jnp.zeros_like(s1)
        for a in range(PEER_TOPK):
            cnt = jnp.where(s1 == t1[a], cnt_by_rank[a:a + 1, :], cnt)
        at_ref[h] = _twin_bf16(jnp.exp(s1 - t1[0]) * (0.5 / zsum))
        cnt_ref[h] = _twin_bf16(cnt * GATE_STEP)
        r2_ref[h] = rank2 * GATE_STEP
        bt_ref[h] = jnp.exp(s2 - t2[0])


def _mid(xf, o_att, o_ssd, mod3, g_attn, g_post_mix, g_pre_ffn, w_out, w_pq, sub_keys, seq_len, tm):
    n, d = xf.shape
    tiles_per_seq = seq_len // tm
    row = lambda i: (i, 0)
    full = lambda i: (0, 0)
    colblk = lambda i: (0, 0, i)
    gate_u32 = jax.ShapeDtypeStruct((PEER_HEADS, N_KEYS, n), jnp.uint32)
    gate_f32 = jax.ShapeDtypeStruct((PEER_HEADS, N_KEYS, n), F32)
    gate_spec = pl.BlockSpec((PEER_HEADS, N_KEYS, tm), colblk)
    return pl.pallas_call(
        functools.partial(_mid_kernel, d=d),
        grid=(n // tm,),
        in_specs=[pl.BlockSpec((tm, d), row),
                  pl.BlockSpec((tm, D_ATT), row),
                  pl.BlockSpec((tm, D_SSD), row),
                  pl.BlockSpec((1, 1, 6 * d), lambda i: (i // tiles_per_seq, 0, 0)),
                  pl.BlockSpec((1, D_ATT), full),
                  pl.BlockSpec((1, d), full),
                  pl.BlockSpec((1, d), full),
                  pl.BlockSpec(w_out.shape, full),
                  pl.BlockSpec(w_pq.shape, full),
                  pl.BlockSpec(sub_keys.shape, lambda i: (0, 0, 0))],
        out_specs=[pl.BlockSpec((tm, d), row),
                   pl.BlockSpec((d, tm), lambda i: (0, i)),
                   gate_spec, gate_spec, gate_spec, gate_spec],
        out_shape=[jax.ShapeDtypeStruct((n, d), F32),
                   jax.ShapeDtypeStruct((d, n), BF16),
                   gate_u32, gate_u32, gate_f32, gate_f32],
        compiler_params=_cparams(("arbitrary",)),
        name="mid",
    )(xf, o_att, o_ssd, mod3, g_attn, g_post_mix, g_pre_ffn, w_out, w_pq, sub_keys)


GELU_C = 0.7978845608028654
PEER_SUB = 256
PEER_SUBS_PER_ACC = 2
PEER_TE = 2048


def _twin_rows(word_row, rows):
    tile = pltpu.bitcast(jnp.broadcast_to(word_row, (SUBLANES, word_row.shape[1])), BF16)
    return jnp.concatenate([tile] * (rows // PACKED_ROWS), axis=0)


def _gelu_tanh_x2(x):
    return x * (1.0 + jnp.tanh(x * ((x * x) * (GELU_C * 0.044715) + GELU_C)))


def _peer_kernel(h2t_ref, u_ref, vt_ref, at_ref, cnt_ref, r2_ref, bt_ref, x1_ref, mod_ref, g_ref,
                 o_ref, acc, r2s, bts, *, d, tm, te, seq_rows):
    j = pl.program_id(1)

    @pl.when(j == 0)
    def _():
        acc[...] = jnp.zeros_like(acc)
        for h in range(PEER_HEADS):
            r2s[h] = r2_ref[h].astype(BF16)
            bts[h] = bt_ref[h].astype(BF16)

    i_per_sub = PEER_SUB // N_KEYS
    n_sub = te // PEER_SUB

    def scores(s):
        return jnp.dot(u_ref[s * PEER_SUB:(s + 1) * PEER_SUB, :], h2t_ref[...],
                       preferred_element_type=F32)

    sc = scores(0)
    pending = []
    for s in range(n_sub):
        sc_next = scores(s + 1) if s + 1 < n_sub else None
        act = _gelu_tanh_x2(sc.astype(BF16))
        lane_blocks = []
        for lg in range(tm // LANES):
            lanes = slice(lg * LANES, (lg + 1) * LANES)
            gsums = [None] * i_per_sub
            for h in range(PEER_HEADS):
                r2 = r2s[h, :, lanes]
                b = bts[h, :, lanes]
                for il in range(i_per_sub):
                    ii = s * i_per_sub + il
                    cnt = _twin_rows(cnt_ref[h, ii:ii + 1, lanes], N_KEYS)
                    a = _twin_rows(at_ref[h, ii:ii + 1, lanes], N_KEYS)
                    term = jnp.minimum(b * a, jnp.maximum(cnt - r2, 0.0))
                    gsums[il] = term if gsums[il] is None else gsums[il] + term
            lane_blocks.append(jnp.concatenate(
                [act[il * N_KEYS:(il + 1) * N_KEYS, lanes] * gsums[il] for il in range(i_per_sub)],
                axis=0))
        pending.append(jnp.concatenate(lane_blocks, axis=1))
        if len(pending) == PEER_SUBS_PER_ACC or s + 1 == n_sub:
            first = s + 1 - len(pending)
            acc[...] += jnp.dot(vt_ref[:, first * PEER_SUB:(s + 1) * PEER_SUB],
                                jnp.concatenate(pending, axis=0), preferred_element_type=F32)
            pending = []
        sc = sc_next

    @pl.when(j == pl.num_programs(1) - 1)
    def _():
        f = _rms(acc[...].T) * g_ref[...]
        for s in range(tm // seq_rows):
            rows = slice(s * seq_rows, (s + 1) * seq_rows)
            gt2 = mod_ref[s][:, 5 * d:6 * d]
            o_ref[rows, :] = x1_ref[rows, :] + gt2 * f[rows, :]


def _peer(h2t, u_bf, vt_bf, at, cnt, r2, bt, x1, mod3, g_post_ffn, seq_len, tm, te):
    d, n = h2t.shape
    n_exp = u_bf.shape[0]
    seq_rows = min(seq_len, tm)
    seqs_per_tile = tm // seq_rows
    gate_spec = pl.BlockSpec((PEER_HEADS, N_KEYS, tm), lambda i, j: (0, 0, i))
    irow_spec = pl.BlockSpec((PEER_HEADS, te // N_KEYS, tm), lambda i, j: (0, j, i))
    assert (te // N_KEYS) % SUBLANES == 0
    if seqs_per_tile == 1:
        tiles_per_seq = seq_len // tm
        mod_spec = pl.BlockSpec((1, 1, 6 * d), lambda i, j: (i // tiles_per_seq, 0, 0))
    else:
        mod_spec = pl.BlockSpec((seqs_per_tile, 1, 6 * d), lambda i, j: (i, 0, 0))
    return pl.pallas_call(
        functools.partial(_peer_kernel, d=d, tm=tm, te=te, seq_rows=seq_rows),
        grid=(n // tm, n_exp // te),
        in_specs=[pl.BlockSpec((d, tm), lambda i, j: (0, i)),
                  pl.BlockSpec((te, d), lambda i, j: (j, 0)),
                  pl.BlockSpec((d, te), lambda i, j: (0, j)),
                  irow_spec, irow_spec, gate_spec, gate_spec,
                  pl.BlockSpec((tm, d), lambda i, j: (i, 0)),
                  mod_spec,
                  pl.BlockSpec((1, d), lambda i, j: (0, 0))],
        out_specs=pl.BlockSpec((tm, d), lambda i, j: (i, 0)),
        out_shape=jax.ShapeDtypeStruct((n, d), F32),
        scratch_shapes=[pltpu.VMEM((d, tm), F32),
                        pltpu.VMEM((PEER_HEADS, N_KEYS, tm), BF16),
                        pltpu.VMEM((PEER_HEADS, N_KEYS, tm), BF16)],
        compiler_params=_cparams(("arbitrary", "arbitrary")),
        name="peer",
    )(h2t, u_bf, vt_bf, at, cnt, r2, bt, x1, mod3, g_post_ffn)


def _pick_tile(seq_len, target):
    t = min(seq_len, target)
    assert seq_len % t == 0 and t % ROW_TILE == 0
    return t


def _layer(x, mod, k_past, v_past, ssm0, conv0, valid, wts):
    nb, seq_len, d = x.shape
    n = nb * seq_len
    xf = x.reshape(n, d)
    mod3 = mod.reshape(nb, 1, 6 * d)

    q, k, v, z, xbc, dt, k_heads, v_heads = _inproj(xf, mod3, wts["g_pre_mix"], wts["w_in"], seq_len,
                                  _pick_tile(seq_len, INPROJ_TM))
    tq = _pick_tile(seq_len, ATTN_TQ)
    if k_past is None:
        o_att = _attention(q, k, v, k, v, nb, seq_len, seq_len, True, tq, min(tq, ATTN_TK))
    else:
        past = k_past.shape[1]
        o_att = _attention(q, k, v, k_past.reshape(nb * past, D_ATT), v_past.reshape(nb * past, D_ATT),
                           nb, seq_len, past, False, tq, ATTN_TK)

    conv0_p = jnp.pad(conv0, ((0, 0), (SUBLANES - (CONV_W - 1), 0), (0, 0)))
    chunk = _pick_tile(seq_len, SSD_CHUNK)
    assert valid == seq_len or seq_len == chunk
    o_ssd, ssm_new = _ssd(xbc, dt, z, conv0_p, ssm0, wts["conv_w"], wts["conv_b"], wts["dt_bias"],
                          wts["a_log"], wts["d_skip"], wts["g_ssd_norm"], nb, seq_len, chunk,
                          min(valid, chunk))

    x1, h2t, at, cnt, r2, bt = _mid(xf, o_att, o_ssd, mod3, wts["g_attn_norm"], wts["g_post_mix"],
                                    wts["g_pre_ffn"], wts["w_out"], wts["w_pq"], wts["sub_keys"],
                                    seq_len, _pick_tile(seq_len, MID_TM))
    spans_ok = seq_len % PEER_TM == 0 or PEER_TM % seq_len == 0
    tm = PEER_TM if n % PEER_TM == 0 and spans_ok else seq_len
    y = _peer(h2t, wts["u_tab"], wts["vt_tab"], at, cnt, r2, bt, x1, mod3, wts["g_post_ffn"],
              seq_len, tm, PEER_TE)

    assert valid >= CONV_W - 1
    conv_new = xbc.reshape(nb, seq_len, CONV_DIM)[:, valid - (CONV_W - 1):valid]
    return (y.reshape(nb, seq_len, d)[:, :valid],
            k_heads.reshape(nb, seq_len, H_ATT, HEAD_DIM)[:, :valid],
            v_heads.reshape(nb, seq_len, H_ATT, HEAD_DIM)[:, :valid],
            ssm_new, conv_new)


def _pad_lanes(a, width):
    return jnp.pad(a, ((0, 0), (0, width - a.shape[1])))


def kernel(x_prompt, x_sample, c_prompt, c_sample, cache_k, cache_v, state_ssm, state_conv, w_ada, b_ada, g_pre_mix, g_post_mix, g_pre_ffn, g_post_ffn, w_in, conv_w, conv_b, dt_bias, a_log, d_skip, g_attn_norm, g_ssd_norm, w_out, w_pq, sub_keys, u_tab, v_tab):
    depth = w_ada.shape[0]
    bp, _, d = x_prompt.shape
    bs, dec_len, _ = x_sample.shape
    d_in = w_in.shape[2]
    d_in_p = -(-d_in // LANES) * LANES
    pad_len = -(-dec_len // ROW_TILE) * ROW_TILE

    xp = x_prompt
    xq = jnp.pad(x_sample, ((0, 0), (0, pad_len - dec_len), (0, 0)))
    outs = [[] for _ in range(8)]
    for i in range(depth):
        wts = dict(
            g_pre_mix=g_pre_mix[i][None], g_post_mix=g_post_mix[i][None],
            g_pre_ffn=g_pre_ffn[i][None], g_post_ffn=g_post_ffn[i][None],
            w_in=_pad_lanes(w_in[i], d_in_p).astype(BF16),
            conv_w=conv_w[i].reshape(CONV_W, CONV_DIM), conv_b=conv_b[i][None],
            dt_bias=_pad_lanes(dt_bias[i][None], LANES), a_log=_pad_lanes(a_log[i][None], LANES),
            d_skip=jnp.repeat(d_skip[i], SSD_P)[None],
            g_attn_norm=g_attn_norm[i][None], g_ssd_norm=g_ssd_norm[i][None],
            w_out=w_out[i].astype(BF16), w_pq=w_pq[i].astype(BF16),
            sub_keys=sub_keys[i].astype(BF16),
            u_tab=u_tab[i].astype(BF16), vt_tab=v_tab[i].T.astype(BF16),
        )
        mod = _ada(jnp.concatenate([c_prompt, c_sample], axis=0), w_ada[i], b_ada[i])
        ssm_zero = jnp.zeros((bp, H_SSD, SSD_P, SSD_N), F32)
        conv_zero = jnp.zeros((bp, CONV_W - 1, CONV_DIM), F32)
        xp, k1, v1, s1, c1 = _layer(xp, mod[:bp], None, None, ssm_zero, conv_zero,
                                    xp.shape[1], wts)
        xq_full, k2, v2, s2, c2 = _layer(xq, mod[bp:], cache_k[i], cache_v[i], state_ssm[i],
                                         state_conv[i], dec_len, wts)
        xq = jnp.pad(xq_full, ((0, 0), (0, pad_len - dec_len), (0, 0))) if i + 1 < depth else xq_full
        for lst, val in zip(outs, (k1, v1, s1, c1, k2, v2, s2, c2)):
            lst.append(val)
    return (xp, xq) + tuple(jnp.stack(o) for o in outs)
```

```python
import functools

import jax
import jax.numpy as jnp
from jax import lax
from jax.experimental import pallas as pl
from jax.experimental.pallas import tpu as pltpu

F32 = jnp.float32
BF16 = jnp.bfloat16
EPS = 1e-6

H_ATT = 8
HEAD_DIM = 64
D_ATT = H_ATT * HEAD_DIM
H_SSD = 8
SSD_P = 64
D_SSD = H_SSD * SSD_P
SSD_GROUPS = 2
HEADS_PER_GROUP = H_SSD // SSD_GROUPS
SSD_N = 64
CONV_W = 4
CONV_DIM = D_SSD + 2 * SSD_GROUPS * SSD_N
N_KEYS = 128
PEER_HEADS = 8
PEER_TOPK = 16
PEER_HALF = 64

LANES = 128
SUBLANES = 8
PACKED_ROWS = 2 * SUBLANES
ROW_TILE = 128
VMEM_LIMIT = 56 * 1024 * 1024

NT_DIMS = (((1,), (1,)), ((), ()))
LOG2E = 1.4426950408889634
MASKED_SCORE = -1e30
ATTN_TQ = 1024
ATTN_TK = 256
ATTN_UNROLL = 4
SSD_CHUNK = 256
INPROJ_TM = 512
MID_TM = 256
PEER_TM = 512


def _cparams(sem):
    return pltpu.CompilerParams(dimension_semantics=sem, vmem_limit_bytes=VMEM_LIMIT)


def _split3(a):
    hi = a.astype(BF16)
    r = a - hi.astype(F32)
    mid = r.astype(BF16)
    lo = (r - mid.astype(F32)).astype(BF16)
    return hi, mid, lo


def _dot_f32(a, b):
    a0, a1, a2 = _split3(a)
    b0, b1, b2 = _split3(b)
    d = functools.partial(jnp.dot, preferred_element_type=F32)
    return (d(a0, b0) + (d(a0, b1) + d(a1, b0))
            + (d(a0, b2) + d(a1, b1) + d(a2, b0)))


def _dot_01(m01, x):
    x0, x1, x2 = _split3(x)
    d = functools.partial(jnp.dot, preferred_element_type=F32)
    return d(m01, x0) + d(m01, x1) + d(m01, x2)


def _sigmoid(x):
    return 1.0 / (1.0 + jnp.exp(-x))


def _softplus(x):
    return jnp.maximum(x, 0.0) + jnp.log(1.0 + jnp.exp(-jnp.abs(x)))


def _rms(x):
    return x * lax.rsqrt(jnp.mean(x * x, axis=-1, keepdims=True) + EPS)


def _ada_kernel(c_ref, w_ref, b_ref, o_ref):
    c = c_ref[...]
    o_ref[...] = _dot_f32(c * _sigmoid(c), w_ref[...]) + b_ref[...]


def _ada(c, w_ada, b_ada):
    nb, d = c.shape
    n_out = w_ada.shape[1]
    bn = d
    return pl.pallas_call(
        _ada_kernel,
        grid=(n_out // bn,),
        in_specs=[pl.BlockSpec((nb, d), lambda j: (0, 0)),
                  pl.BlockSpec((d, bn), lambda j: (0, j)),
                  pl.BlockSpec((1, bn), lambda j: (0, j))],
        out_specs=pl.BlockSpec((nb, bn), lambda j: (0, j)),
        out_shape=jax.ShapeDtypeStruct((nb, n_out), F32),
        compiler_params=_cparams(("arbitrary",)),
        name="ada",
    )(c, w_ada, b_ada.reshape(1, n_out))


def _inproj_kernel(x_ref, mod_ref, g_ref, w_ref, q_ref, k_ref, v_ref, z_ref, xbc_ref, dt_ref,
                   kh_ref, vh_ref, *, d):
    mod = mod_ref[0]
    sh1 = mod[:, 0:d]
    sc1 = mod[:, d:2 * d]
    h = (_rms(x_ref[...]) * g_ref[...] * (1.0 + sc1) + sh1).astype(BF16)

    def proj(lo, hi):
        return jnp.dot(h, w_ref[:, lo:hi], preferred_element_type=F32)

    o = 0
    q_ref[...] = (proj(o, o + D_ATT) * (HEAD_DIM ** -0.5 * LOG2E)).astype(BF16)
    o += D_ATT
    for dense_ref, heads_ref in ((k_ref, kh_ref), (v_ref, vh_ref)):
        p = proj(o, o + D_ATT)
        dense_ref[...] = p.astype(BF16)
        for hd in range(H_ATT):
            heads_ref[pl.ds(hd, p.shape[0], stride=H_ATT), :] = p[:, hd * HEAD_DIM:(hd + 1) * HEAD_DIM]
        o += D_ATT
    z_ref[...] = proj(o, o + D_SSD)
    o += D_SSD
    xbc_ref[...] = proj(o, o + CONV_DIM)
    o += CONV_DIM
    dt_ref[...] = proj(o, o + LANES)


def _inproj(xf, mod3, g_pre, w_in_p, seq_len, tm):
    n, d = xf.shape
    tiles_per_seq = seq_len // tm
    row = lambda i: (i, 0)
    full = lambda i: (0, 0)
    widths = (D_ATT, D_ATT, D_ATT, D_SSD, CONV_DIM, LANES)
    dtypes = (BF16, BF16, BF16, F32, F32, F32)
    heads_spec = pl.BlockSpec((tm * H_ATT, HEAD_DIM), row)
    heads_shape = jax.ShapeDtypeStruct((n * H_ATT, HEAD_DIM), F32)
    return pl.pallas_call(
        functools.partial(_inproj_kernel, d=d),
        grid=(n // tm,),
        in_specs=[pl.BlockSpec((tm, d), row),
                  pl.BlockSpec((1, 1, 6 * d), lambda i: (i // tiles_per_seq, 0, 0)),
                  pl.BlockSpec((1, d), full),
                  pl.BlockSpec(w_in_p.shape, full)],
        out_specs=[pl.BlockSpec((tm, w), row) for w in widths] + [heads_spec, heads_spec],
        out_shape=[jax.ShapeDtypeStruct((n, w), t) for w, t in zip(widths, dtypes)]
        + [heads_shape, heads_shape],
        compiler_params=_cparams(("arbitrary",)),
        name="inproj",
    )(xf, mod3, g_pre, w_in_p)


def _attn_kernel(q_ref, kd_ref, vd_ref, kp_ref, vp_ref, o_ref, kbf, vbf, oacc, carry, *,
                 tq, tk, n_prev_static):
    qi = pl.program_id(2)
    lp = kp_ref.shape[0]
    cast_rows = min(lp, 512)

    prev_reused = n_prev_static is None

    if prev_reused:
        @pl.when(qi == 0)
        def _():
            def cp(c, _):
                rows = pl.ds(pl.multiple_of(c * cast_rows, cast_rows), cast_rows)
                for h in range(2):
                    lanes = slice(h * HEAD_DIM, (h + 1) * HEAD_DIM)
                    kbf[h, rows, :] = kp_ref[rows, lanes].astype(BF16)
                    vbf[h, rows, :] = vp_ref[rows, lanes].astype(BF16)
                return 0
            lax.fori_loop(0, lp // cast_rows, cp, 0)

    td = min(tq, tk)
    kk = lax.broadcasted_iota(jnp.int32, (tk, tk), 0)
    nn = lax.broadcasted_iota(jnp.int32, (tk, tk), 1)
    msum = jnp.where(kk > nn, 1.0, 0.0).astype(BF16)

    oacc[...] = jnp.zeros_like(oacc)
    carry[...] = jnp.zeros_like(carry)

    def step(h, rows, z2, v):
        nk = z2.shape[1]
        zb = z2.astype(BF16)
        l2 = jnp.log(1.0 + jnp.exp2(-jnp.abs(zb))) * LOG2E
        soft = jnp.maximum(zb, 0.0) + l2
        log_beta = jnp.minimum(zb, 0.0) - l2
        newer = jnp.dot(soft, msum[:nk, :nk], preferred_element_type=F32)
        c = carry[h, rows, :]
        gone = (newer + jnp.concatenate([c] * (nk // LANES), axis=1)).astype(BF16)
        w = jnp.exp2(log_beta - gone)
        carry[h, rows, :] = c + (newer[:, 0:1] + soft[:, 0:1].astype(F32))
        oacc[h, rows, :] += jnp.dot(w, v, preferred_element_type=F32)

    for jb in reversed(range(tq // td)):
        r0 = jb * td
        rows = slice(r0, tq)
        keys = slice(r0, r0 + td)
        rl = lax.broadcasted_iota(jnp.int32, (tq - r0, td), 0)
        cl = lax.broadcasted_iota(jnp.int32, (tq - r0, td), 1)
        causal = cl < rl
        for h in range(2):
            lanes = slice(h * HEAD_DIM, (h + 1) * HEAD_DIM)
            z2 = lax.dot_general(q_ref[rows, lanes], kd_ref[keys, lanes].astype(BF16), NT_DIMS,
                                 preferred_element_type=F32)
            step(h, rows, jnp.where(causal, z2, MASKED_SCORE), vd_ref[keys, lanes].astype(BF16))

    n_prev = qi * (tq // tk) if n_prev_static is None else n_prev_static
    prev_multiple = (tq // tk) if n_prev_static is None else n_prev_static
    unroll = next(u for u in (ATTN_UNROLL, 2, 1) if prev_multiple % u == 0)

    def body(jj, _):
        for u in range(unroll):
            keys = pl.ds(pl.multiple_of((n_prev - 1 - unroll * jj - u) * tk, tk), tk)
            for h in range(2):
                lanes = slice(h * HEAD_DIM, (h + 1) * HEAD_DIM)
                if prev_reused:
                    kb, vb = kbf[h, keys, :], vbf[h, keys, :]
                else:
                    kb = kp_ref[keys, lanes].astype(BF16)
                    vb = vp_ref[keys, lanes].astype(BF16)
                z2 = lax.dot_general(q_ref[:, lanes], kb, NT_DIMS, preferred_element_type=F32)
                step(h, slice(0, tq), z2, vb)
        return 0

    lax.fori_loop(0, n_prev // unroll, body, 0)
    o_ref[...] = jnp.concatenate([oacc[0], oacc[1]], axis=1)


def _attention(q, k, v, k_prev, v_prev, nb, seq_len, prev_len, prev_is_self, tq, tk):
    n = q.shape[0]
    nq = seq_len // tq
    assert seq_len % tq == 0 and prev_len % tk == 0 and (tq % tk == 0 or tk % tq == 0)
    blk = lambda b, hp, qi: (b * nq + qi, hp)
    prev = lambda b, hp, qi: (b, hp)
    kern = functools.partial(_attn_kernel, tq=tq, tk=tk,
                             n_prev_static=None if prev_is_self else prev_len // tk)
    return pl.pallas_call(
        kern,
        grid=(nb, H_ATT // 2, nq),
        in_specs=[pl.BlockSpec((tq, LANES), blk),
                  pl.BlockSpec((tq, LANES), blk),
                  pl.BlockSpec((tq, LANES), blk),
                  pl.BlockSpec((prev_len, LANES), prev),
                  pl.BlockSpec((prev_len, LANES), prev)],
        out_specs=pl.BlockSpec((tq, LANES), blk),
        out_shape=jax.ShapeDtypeStruct((n, D_ATT), F32),
        scratch_shapes=[pltpu.VMEM((2, prev_len, HEAD_DIM), BF16),
                        pltpu.VMEM((2, prev_len, HEAD_DIM), BF16),
                        pltpu.VMEM((2, tq, HEAD_DIM), F32),
                        pltpu.VMEM((2, tq, LANES), F32)],
        compiler_params=_cparams(("arbitrary", "arbitrary", "arbitrary")),
        name="attn",
    )(q, k, v, k_prev, v_prev)


def _ssd_kernel(xbc_ref, dt_ref, z_ref, conv0_ref, ssm0_ref, cw_ref, cb_ref, dtb_ref, alog_ref,
                dsk_ref, g_ref, y_ref, ssm_ref, buf, state, *, q, valid):
    ci = pl.program_id(1)
    hist = SUBLANES

    @pl.when(ci == 0)
    def _():
        buf[0:hist, :] = conv0_ref[0]
        state[...] = ssm0_ref[0]

    buf[hist:hist + q, :] = xbc_ref[...]
    cw = cw_ref[...]
    acc = cb_ref[...] + cw[CONV_W - 1:CONV_W, :] * buf[hist:hist + q, :]
    for s in range(1, CONV_W):
        acc = acc + cw[CONV_W - 1 - s:CONV_W - s, :] * buf[hist - s:hist - s + q, :]
    xc = acc * _sigmoid(acc)
    buf[0:hist, :] = buf[q:q + hist, :]

    lane = lax.broadcasted_iota(jnp.int32, (q, LANES), 1)
    rowi = lax.broadcasted_iota(jnp.int32, (q, LANES), 0)
    dt = _softplus(dt_ref[...] + dtb_ref[...])
    dt = jnp.where(lane < H_SSD, dt, 0.0)
    if valid < q:
        dt = jnp.where(rowi < valid, dt, 0.0)
    d_a = dt * (-jnp.exp(alog_ref[...]))

    tt = lax.broadcasted_iota(jnp.int32, (q, q), 0)
    ss = lax.broadcasted_iota(jnp.int32, (q, q), 1)
    causal = ss <= tt
    cum = _dot_01(jnp.where(causal, 1.0, 0.0).astype(BF16), d_a)
    cum_t = cum.T
    last = cum[valid - 1:valid, :]
    e_cum = jnp.exp(cum)
    e_end = jnp.exp(last - cum)
    e_last = jnp.exp(last)

    ys = []
    for g in range(SSD_GROUPS):
        b_off = D_SSD + g * SSD_N
        c_off = D_SSD + SSD_GROUPS * SSD_N + g * SSD_N
        bg = xc[:, b_off:b_off + SSD_N].astype(BF16)
        cg = xc[:, c_off:c_off + SSD_N].astype(BF16)
        cb = lax.dot_general(cg, bg, NT_DIMS, preferred_element_type=F32)
        for pair in range(HEADS_PER_GROUP // 2):
            xws = []
            for hh in range(2):
                h = g * HEADS_PER_GROUP + pair * 2 + hh
                xh = xc[:, h * SSD_P:(h + 1) * SSD_P]
                seg = cum[:, h:h + 1] - cum_t[h:h + 1, :]
                decay = jnp.exp(jnp.where(causal, seg, -jnp.inf))
                xdt = xh * dt[:, h:h + 1]
                y_diag = jnp.dot((cb * decay).astype(BF16), xdt.astype(BF16),
                                 preferred_element_type=F32)
                st = state[h]
                y_off = lax.dot_general(cg, st.astype(BF16), NT_DIMS,
                                        preferred_element_type=F32) * e_cum[:, h:h + 1]
                ys.append(y_diag + y_off + dsk_ref[:, h * SSD_P:(h + 1) * SSD_P] * xh)
                xws.append(xdt * e_end[:, h:h + 1])
            xw_t = jnp.concatenate(xws, axis=1).T.astype(BF16)
            for hh in range(2):
                h = g * HEADS_PER_GROUP + pair * 2 + hh
                upd = jnp.dot(xw_t[hh * SSD_P:(hh + 1) * SSD_P, :], bg, preferred_element_type=F32)
                state[h] = state[h] * e_last[:, h:h + 1] + upd
    y = jnp.concatenate(ys, axis=1)
    zz = z_ref[...]
    y_ref[...] = _rms(y * (zz * _sigmoid(zz))) * g_ref[...]

    @pl.when(ci == pl.num_programs(1) - 1)
    def _():
        ssm_ref[0] = state[...]


def _ssd(xbc, dt, z, conv0_p, ssm0, cw, cb, dtb, alog, dsk, g_ssd, nb, seq_len, q, valid):
    n = xbc.shape[0]
    nc = seq_len // q
    row = lambda b, c: (b * nc + c, 0)
    full2 = lambda b, c: (0, 0)
    return pl.pallas_call(
        functools.partial(_ssd_kernel, q=q, valid=valid),
        grid=(nb, nc),
        in_specs=[pl.BlockSpec((q, CONV_DIM), row),
                  pl.BlockSpec((q, LANES), row),
                  pl.BlockSpec((q, D_SSD), row),
                  pl.BlockSpec((1, SUBLANES, CONV_DIM), lambda b, c: (b, 0, 0)),
                  pl.BlockSpec((1, H_SSD, SSD_P, SSD_N), lambda b, c: (b, 0, 0, 0)),
                  pl.BlockSpec((CONV_W, CONV_DIM), full2),
                  pl.BlockSpec((1, CONV_DIM), full2),
                  pl.BlockSpec((1, LANES), full2),
                  pl.BlockSpec((1, LANES), full2),
                  pl.BlockSpec((1, D_SSD), full2),
                  pl.BlockSpec((1, D_SSD), full2)],
        out_specs=[pl.BlockSpec((q, D_SSD), row),
                   pl.BlockSpec((1, H_SSD, SSD_P, SSD_N), lambda b, c: (b, 0, 0, 0))],
        out_shape=[jax.ShapeDtypeStruct((n, D_SSD), F32),
                   jax.ShapeDtypeStruct((nb, H_SSD, SSD_P, SSD_N), F32)],
        scratch_shapes=[pltpu.VMEM((q + SUBLANES, CONV_DIM), F32),
                        pltpu.VMEM((H_SSD, SSD_P, SSD_N), F32)],
        compiler_params=_cparams(("arbitrary", "arbitrary")),
        name="ssd",
    )(xbc, dt, z, conv0_p, ssm0, cw, cb, dtb, alog, dsk, g_ssd)


def _twin_bf16(x):
    bits = pltpu.bitcast(x.astype(BF16).astype(F32), jnp.uint32)
    return bits | (bits >> 16)


GATE_STEP = 2.0 ** 64


def _merge_exchange_network(n):
    t = (n - 1).bit_length()
    p = 1 << (t - 1)
    pairs = []
    while p > 0:
        q, r, d = 1 << (t - 1), 0, p
        while True:
            pairs += [(i, i + d) for i in range(n - d) if (i & p) == r]
            if q == p:
                break
            d, q, r = q - p, q >> 1, p
        p >>= 1
    return pairs


def _sorted_top(s, count):
    assert s.shape[0] == count * SUBLANES and count & (count - 1) == 0
    slabs = [s[k * SUBLANES:(k + 1) * SUBLANES, :] for k in range(count)]
    for i, j in _merge_exchange_network(count):
        slabs[i], slabs[j] = jnp.maximum(slabs[i], slabs[j]), jnp.minimum(slabs[i], slabs[j])
    shift = SUBLANES // 2
    while shift >= 1:
        partner = [pltpu.roll(x, shift, axis=0) for x in slabs]
        slabs = [jnp.maximum(slabs[i], partner[count - 1 - i]) for i in range(count)]
        stride = count // 2
        while stride >= 1:
            for i in range(count):
                if (i // stride) % 2 == 0:
                    j = i + stride
                    slabs[i], slabs[j] = (jnp.maximum(slabs[i], slabs[j]),
                                          jnp.minimum(slabs[i], slabs[j]))
            stride //= 2
        shift //= 2
    return [x[0:1, :] for x in slabs]


def _mid_kernel(x_ref, oa_ref, os_ref, mod_ref, ga_ref, gpm_ref, gpf_ref, wo_ref, wpq_ref, sk_ref,
                x1_ref, h2t_ref, at_ref, cnt_ref, r2_ref, bt_ref, *, d):
    mod = mod_ref[0]
    gt1 = mod[:, 2 * d:3 * d]
    sh2 = mod[:, 3 * d:4 * d]
    sc2 = mod[:, 4 * d:5 * d]
    attn_o = (_rms(oa_ref[...]) * ga_ref[...]).astype(BF16)
    mix = (jnp.dot(attn_o, wo_ref[0:D_ATT, :], preferred_element_type=F32)
           + jnp.dot(os_ref[...].astype(BF16), wo_ref[D_ATT:D_ATT + D_SSD, :],
                     preferred_element_type=F32))
    x1 = x_ref[...] + gt1 * (_rms(mix) * gpm_ref[...])
    x1_ref[...] = x1
    h2 = _rms(x1) * gpf_ref[...] * (1.0 + sc2) + sh2
    h2t_ref[...] = h2.T.astype(BF16)
    qh = jnp.dot(h2.astype(BF16), wpq_ref[...], preferred_element_type=F32).astype(BF16)
    k1 = sk_ref[0]
    k2 = sk_ref[1]
    for h in range(PEER_HEADS):
        base = h * 2 * PEER_HALF
        s1 = lax.dot_general(k1, qh[:, base:base + PEER_HALF], NT_DIMS,
                             preferred_element_type=F32)
        s2 = lax.dot_general(k2, qh[:, base + PEER_HALF:base + 2 * PEER_HALF], NT_DIMS,
                             preferred_element_type=F32)
        t1 = _sorted_top(s1, PEER_TOPK)
        t2 = _sorted_top(s2, PEER_TOPK)
        rank2 = jnp.zeros_like(s2)
        for b in range(PEER_TOPK):
            rank2 = jnp.where(t2[b] > s2, float(b + 1), rank2)
        half = PEER_TOPK // 2
        t1_lo = jnp.concatenate(t1[:half], axis=0)
        t1_hi = jnp.concatenate(t1[half:], axis=0)
        t2_hi = jnp.concatenate(t2[half:], axis=0)
        rank_a = lax.broadcasted_iota(jnp.int32, t1_lo.shape, 0)
        blocks = [t1_lo + t2[0], t1_hi + t2[0], t1[0] + t2_hi]
        for b in range(1, half):
            blocks.append(jnp.where(rank_a < PEER_TOPK // (b + 1), t1_lo + t2[b], -jnp.inf))
        filler = [jnp.full_like(t1_lo, -jnp.inf)] * (PEER_TOPK - len(blocks))
        best = _sorted_top(jnp.concatenate(blocks + filler, axis=0), PEER_TOPK)
        tau = best[PEER_TOPK - 1]
        zsum = jnp.ones_like(best[0])
        for r in range(1, PEER_TOPK):
            zsum = zsum + jnp.exp(best[r] - best[0])
        taken = [jnp.where(blk >= tau, 1.0, 0.0) for blk in blocks]
        cnt_lo = taken[0]
        for blk in taken[3:]:
            cnt_lo = cnt_lo + blk
        a0_extra = jnp.sum(taken[2], axis=0, keepdims=True)
        cnt_lo = cnt_lo + jnp.where(rank_a == 0, a0_extra, 0.0)
        cnt_by_rank = jnp.concatenate([cnt_lo, taken[1]], axis=0)
        cnt = jnp.zeros_like(s1)
        for a in range(PEER_TOPK):
            cnt = jnp.where(s1 == t1[a], cnt_by_rank[a:a + 1, :], cnt)
        at_ref[h] = _twin_bf16(jnp.exp(s1 - t1[0]) * (0.5 / zsum))
        cnt_ref[h] = _twin_bf16(cnt * GATE_STEP)
        r2_ref[h] = rank2 * GATE_STEP
        bt_ref[h] = jnp.exp(s2 - t2[0])


def _mid(xf, o_att, o_ssd, mod3, g_attn, g_post_mix, g_pre_ffn, w_out, w_pq, sub_keys, seq_len, tm):
    n, d = xf.shape
    tiles_per_seq = seq_len // tm
    row = lambda i: (i, 0)
    full = lambda i: (0, 0)
    colblk = lambda i: (0, 0, i)
    gate_u32 = jax.ShapeDtypeStruct((PEER_HEADS, N_KEYS, n), jnp.uint32)
    gate_f32 = jax.ShapeDtypeStruct((PEER_HEADS, N_KEYS, n), F32)
    gate_spec = pl.BlockSpec((PEER_HEADS, N_KEYS, tm), colblk)
    return pl.pallas_call(
        functools.partial(_mid_kernel, d=d),
        grid=(n // tm,),
        in_specs=[pl.BlockSpec((tm, d), row),
                  pl.BlockSpec((tm, D_ATT), row),
                  pl.BlockSpec((tm, D_SSD), row),
                  pl.BlockSpec((1, 1, 6 * d), lambda i: (i // tiles_per_seq, 0, 0)),
                  pl.BlockSpec((1, D_ATT), full),
                  pl.BlockSpec((1, d), full),
                  pl.BlockSpec((1, d), full),
                  pl.BlockSpec(w_out.shape, full),
                  pl.BlockSpec(w_pq.shape, full),
                  pl.BlockSpec(sub_keys.shape, lambda i: (0, 0, 0))],
        out_specs=[pl.BlockSpec((tm, d), row),
                   pl.BlockSpec((d, tm), lambda i: (0, i)),
                   gate_spec, gate_spec, gate_spec, gate_spec],
        out_shape=[jax.ShapeDtypeStruct((n, d), F32),
                   jax.ShapeDtypeStruct((d, n), BF16),
                   gate_u32, gate_u32, gate_f32, gate_f32],
        compiler_params=_cparams(("arbitrary",)),
        name="mid",
    )(xf, o_att, o_ssd, mod3, g_attn, g_post_mix, g_pre_ffn, w_out, w_pq, sub_keys)


GELU_C = 0.7978845608028654
PEER_SUB = 256
PEER_SUBS_PER_ACC = 2
PEER_TE = 2048


def _twin_rows(word_row, rows):
    tile = pltpu.bitcast(jnp.broadcast_to(word_row, (SUBLANES, word_row.shape[1])), BF16)
    return jnp.concatenate([tile] * (rows // PACKED_ROWS), axis=0)


def _gelu_tanh_x2(x):
    return x * (1.0 + jnp.tanh(x * ((x * x) * (GELU_C * 0.044715) + GELU_C)))


def _peer_kernel(h2t_ref, u_ref, vt_ref, at_ref, cnt_ref, r2_ref, bt_ref, x1_ref, mod_ref, g_ref,
                 o_ref, acc, r2s, bts, *, d, tm, te, seq_rows):
    j = pl.program_id(1)

    @pl.when(j == 0)
    def _():
        acc[...] = jnp.zeros_like(acc)
        for h in range(PEER_HEADS):
            r2s[h] = r2_ref[h].astype(BF16)
            bts[h] = bt_ref[h].astype(BF16)

    i_per_sub = PEER_SUB // N_KEYS
    n_sub = te // PEER_SUB

    def scores(s):
        return jnp.dot(u_ref[s * PEER_SUB:(s + 1) * PEER_SUB, :], h2t_ref[...],
                       preferred_element_type=F32)

    sc = scores(0)
    pending = []
    for s in range(n_sub):
        sc_next = scores(s + 1) if s + 1 < n_sub else None
        act = _gelu_tanh_x2(sc.astype(BF16))
        lane_blocks = []
        for lg in range(tm // LANES):
            lanes = slice(lg * LANES, (lg + 1) * LANES)
            gsums = [None] * i_per_sub
            for h in range(PEER_HEADS):
                r2 = r2s[h, :, lanes]
                b = bts[h, :, lanes]
                for il in range(i_per_sub):
                    ii = s * i_per_sub + il
                    cnt = _twin_rows(cnt_ref[h, ii:ii + 1, lanes], N_KEYS)
                    a = _twin_rows(at_ref[h, ii:ii + 1, lanes], N_KEYS)
                    term = jnp.minimum(b * a, jnp.maximum(cnt - r2, 0.0))
                    gsums[il] = term if gsums[il] is None else gsums[il] + term
            lane_blocks.append(jnp.concatenate(
                [act[il * N_KEYS:(il + 1) * N_KEYS, lanes] * gsums[il] for il in range(i_per_sub)],
                axis=0))
        pending.append(jnp.concatenate(lane_blocks, axis=1))
        if len(pending) == PEER_SUBS_PER_ACC or s + 1 == n_sub:
            first = s + 1 - len(pending)
            acc[...] += jnp.dot(vt_ref[:, first * PEER_SUB:(s + 1) * PEER_SUB],
                                jnp.concatenate(pending, axis=0), preferred_element_type=F32)
            pending = []
        sc = sc_next

    @pl.when(j == pl.num_programs(1) - 1)
    def _():
        f = _rms(acc[...].T) * g_ref[...]
        for s in range(tm // seq_rows):
            rows = slice(s * seq_rows, (s + 1) * seq_rows)
            gt2 = mod_ref[s][:, 5 * d:6 * d]
            o_ref[rows, :] = x1_ref[rows, :] + gt2 * f[rows, :]


def _peer(h2t, u_bf, vt_bf, at, cnt, r2, bt, x1, mod3, g_post_ffn, seq_len, tm, te):
    d, n = h2t.shape
    n_exp = u_bf.shape[0]
    seq_rows = min(seq_len, tm)
    seqs_per_tile = tm // seq_rows
    gate_spec = pl.BlockSpec((PEER_HEADS, N_KEYS, tm), lambda i, j: (0, 0, i))
    irow_spec = pl.BlockSpec((PEER_HEADS, te // N_KEYS, tm), lambda i, j: (0, j, i))
    assert (te // N_KEYS) % SUBLANES == 0
    if seqs_per_tile == 1:
        tiles_per_seq = seq_len // tm
        mod_spec = pl.BlockSpec((1, 1, 6 * d), lambda i, j: (i // tiles_per_seq, 0, 0))
    else:
        mod_spec = pl.BlockSpec((seqs_per_tile, 1, 6 * d), lambda i, j: (i, 0, 0))
    return pl.pallas_call(
        functools.partial(_peer_kernel, d=d, tm=tm, te=te, seq_rows=seq_rows),
        grid=(n // tm, n_exp // te),
        in_specs=[pl.BlockSpec((d, tm), lambda i, j: (0, i)),
                  pl.BlockSpec((te, d), lambda i, j: (j, 0)),
                  pl.BlockSpec((d, te), lambda i, j: (0, j)),
                  irow_spec, irow_spec, gate_spec, gate_spec,
                  pl.BlockSpec((tm, d), lambda i, j: (i, 0)),
                  mod_spec,
                  pl.BlockSpec((1, d), lambda i, j: (0, 0))],
        out_specs=pl.BlockSpec((tm, d), lambda i, j: (i, 0)),
        out_shape=jax.ShapeDtypeStruct((n, d), F32),
        scratch_shapes=[pltpu.VMEM((d, tm), F32),
                        pltpu.VMEM((PEER_HEADS, N_KEYS, tm), BF16),
                        pltpu.VMEM((PEER_HEADS, N_KEYS, tm), BF16)],
        compiler_params=_cparams(("arbitrary", "arbitrary")),
        name="peer",
    )(h2t, u_bf, vt_bf, at, cnt, r2, bt, x1, mod3, g_post_ffn)


def _pick_tile(seq_len, target):
    t = min(seq_len, target)
    assert seq_len % t == 0 and t % ROW_TILE == 0
    return t


def _layer(x, mod, k_past, v_past, ssm0, conv0, valid, wts):
    nb, seq_len, d = x.shape
    n = nb * seq_len
    xf = x.reshape(n, d)
    mod3 = mod.reshape(nb, 1, 6 * d)

    q, k, v, z, xbc, dt, k_heads, v_heads = _inproj(xf, mod3, wts["g_pre_mix"], wts["w_in"], seq_len,
                                  _pick_tile(seq_len, INPROJ_TM))
    tq = _pick_tile(seq_len, ATTN_TQ)
    if k_past is None:
        o_att = _attention(q, k, v, k, v, nb, seq_len, seq_len, True, tq, min(tq, ATTN_TK))
    else:
        past = k_past.shape[1]
        o_att = _attention(q, k, v, k_past.reshape(nb * past, D_ATT), v_past.reshape(nb * past, D_ATT),
                           nb, seq_len, past, False, tq, ATTN_TK)

    conv0_p = jnp.pad(conv0, ((0, 0), (SUBLANES - (CONV_W - 1), 0), (0, 0)))
    chunk = _pick_tile(seq_len, SSD_CHUNK)
    assert valid == seq_len or seq_len == chunk
    o_ssd, ssm_new = _ssd(xbc, dt, z, conv0_p, ssm0, wts["conv_w"], wts["conv_b"], wts["dt_bias"],
                          wts["a_log"], wts["d_skip"], wts["g_ssd_norm"], nb, seq_len, chunk,
                          min(valid, chunk))

    x1, h2t, at, cnt, r2, bt = _mid(xf, o_att, o_ssd, mod3, wts["g_attn_norm"], wts["g_post_mix"],
                                    wts["g_pre_ffn"], wts["w_out"], wts["w_pq"], wts["sub_keys"],
                                    seq_len, _pick_tile(seq_len, MID_TM))
    spans_ok = seq_len % PEER_TM == 0 or PEER_TM % seq_len == 0
    tm = PEER_TM if n % PEER_TM == 0 and spans_ok else seq_len
    y = _peer(h2t, wts["u_tab"], wts["vt_tab"], at, cnt, r2, bt, x1, mod3, wts["g_post_ffn"],
              seq_len, tm, PEER_TE)

    assert valid >= CONV_W - 1
    conv_new = xbc.reshape(nb, seq_len, CONV_DIM)[:, valid - (CONV_W - 1):valid]
    return (y.reshape(nb, seq_len, d)[:, :valid],
            k_heads.reshape(nb, seq_len, H_ATT, HEAD_DIM)[:, :valid],
            v_heads.reshape(nb, seq_len, H_ATT, HEAD_DIM)[:, :valid],
            ssm_new, conv_new)


def _pad_lanes(a, width):
    return jnp.pad(a, ((0, 0), (0, width - a.shape[1])))


def kernel(x_prompt, x_sample, c_prompt, c_sample, cache_k, cache_v, state_ssm, state_conv, w_ada, b_ada, g_pre_mix, g_post_mix, g_pre_ffn, g_post_ffn, w_in, conv_w, conv_b, dt_bias, a_log, d_skip, g_attn_norm, g_ssd_norm, w_out, w_pq, sub_keys, u_tab, v_tab):
    depth = w_ada.shape[0]
    bp, _, d = x_prompt.shape
    bs, dec_len, _ = x_sample.shape
    d_in = w_in.shape[2]
    d_in_p = -(-d_in // LANES) * LANES
    pad_len = -(-dec_len // ROW_TILE) * ROW_TILE

    xp = x_prompt
    xq = jnp.pad(x_sample, ((0, 0), (0, pad_len - dec_len), (0, 0)))
    outs = [[] for _ in range(8)]
    for i in range(depth):
        wts = dict(
            g_pre_mix=g_pre_mix[i][None], g_post_mix=g_post_mix[i][None],
            g_pre_ffn=g_pre_ffn[i][None], g_post_ffn=g_post_ffn[i][None],
            w_in=_pad_lanes(w_in[i], d_in_p).astype(BF16),
            conv_w=conv_w[i].reshape(CONV_W, CONV_DIM), conv_b=conv_b[i][None],
            dt_bias=_pad_lanes(dt_bias[i][None], LANES), a_log=_pad_lanes(a_log[i][None], LANES),
            d_skip=jnp.repeat(d_skip[i], SSD_P)[None],
            g_attn_norm=g_attn_norm[i][None], g_ssd_norm=g_ssd_norm[i][None],
            w_out=w_out[i].astype(BF16), w_pq=w_pq[i].astype(BF16),
            sub_keys=sub_keys[i].astype(BF16),
            u_tab=u_tab[i].astype(BF16), vt_tab=v_tab[i].T.astype(BF16),
        )
        mod = _ada(jnp.concatenate([c_prompt, c_sample], axis=0), w_ada[i], b_ada[i])
        ssm_zero = jnp.zeros((bp, H_SSD, SSD_P, SSD_N), F32)
        conv_zero = jnp.zeros((bp, CONV_W - 1, CONV_DIM), F32)
        xp, k1, v1, s1, c1 = _layer(xp, mod[:bp], None, None, ssm_zero, conv_zero,
                                    xp.shape[1], wts)
        xq_full, k2, v2, s2, c2 = _layer(xq, mod[bp:], cache_k[i], cache_v[i], state_ssm[i],
                                         state_conv[i], dec_len, wts)
        xq = jnp.pad(xq_full, ((0, 0), (0, pad_len - dec_len), (0, 0))) if i + 1 < depth else xq_full
        for lst, val in zip(outs, (k1, v1, s1, c1, k2, v2, s2, c2)):
            lst.append(val)
    return (xp, xq) + tuple(jnp.stack(o) for o in outs)
```
